```python
import math
import jax
import jax.numpy as jnp
from jax import lax
import numpy as np

D_MODEL = 2048
BATCH = 16
SEQ = 256
DEPTH = 4
DEC_BATCH = 2
DEC_SEQ = 4096
PAST_LEN = 256

GRID_W = 64
HEAD_DIM = 128
N_HEADS = D_MODEL // HEAD_DIM
DIFF_QK_DIM = HEAD_DIM // 2
GQA_KV_HEADS = N_HEADS // 4
NA_ROWS = 8
NA_COLS = 16
Q_BLOCK = 128
D_FF = D_MODEL * 7 // 2
N_EXPERTS = 8
TOP_K = 2
MOE_BLOCK = 128
N_MIXERS = 3
ROPE_THETA = 10000.0
NORM_EPS = 1e-6
F32 = jnp.float32

kernel_name = "hybrid_dit_diffattn_gqa_natten_moe_step"


def rms_norm(x, gain):
    xf = x.astype(F32)
    y = xf * lax.rsqrt(jnp.mean(xf * xf, axis=-1, keepdims=True) + NORM_EPS)
    return (y * gain.astype(F32)).astype(x.dtype)


def adaln_params(cond, ada_w, ada_b):
    m = jax.nn.silu(cond) @ ada_w + ada_b
    return jnp.split(m[:, None, :], 6, axis=-1)


def modulate(x, gain, shift, scale):
    return rms_norm(x, gain) * (1.0 + scale) + shift


def kv_heads(kind):
    return GQA_KV_HEADS if kind == 1 else N_HEADS


def axial_rope_tables(n_tokens, dim, dtype):
    quarter = dim // 4
    inv_freq = ROPE_THETA ** (-jnp.arange(quarter, dtype=F32) / quarter)
    t = jnp.arange(n_tokens)
    ang_r = (t // GRID_W).astype(F32)[:, None] * inv_freq
    ang_c = (t % GRID_W).astype(F32)[:, None] * inv_freq
    ang = jnp.concatenate([ang_r, ang_r, ang_c, ang_c], axis=-1)
    return jnp.cos(ang).astype(dtype), jnp.sin(ang).astype(dtype)


def apply_axial_rope(x, cos, sin):
    x1, x2, x3, x4 = jnp.split(x, 4, axis=-1)
    rot = jnp.concatenate([-x2, x1, -x4, x3], axis=-1)
    shape = (cos.shape[0],) + (1,) * (x.ndim - 3) + (cos.shape[-1],)
    return x * cos.reshape(shape) + rot * sin.reshape(shape)


def swap_seq_heads(x):
    return x.transpose(0, 2, 1, 3)


def sweep_query_blocks(fn, q):
    b, t = q.shape[:2]
    nb = t // Q_BLOCK
    qb = jnp.moveaxis(q.reshape((b, nb, Q_BLOCK) + q.shape[2:]), 1, 0)
    o = lax.map(fn, qb)
    return jnp.moveaxis(o, 0, 1).reshape((b, t) + o.shape[3:])


def gqa_attend(q, k, v):
    b, nq, h, dh = q.shape
    hkv = k.shape[2]
    qg = q.reshape(b, nq, hkv, h // hkv, dh)
    s = jnp.einsum("bqkgd,bskd->bkgqs", qg, k, preferred_element_type=F32) * (dh ** -0.5)
    p = jax.nn.softmax(s, axis=-1).astype(v.dtype)
    o = jnp.einsum("bkgqs,bskd->bqkgd", p, v)
    return o.reshape(b, nq, h, dh)


def diff_attend(q, k, v, lam):
    s = jnp.einsum("bqhmd,bshmd->bhmqs", q, k, preferred_element_type=F32) * (q.shape[-1] ** -0.5)
    p = jax.nn.softmax(s, axis=-1)
    a = (p[:, :, 0] - lam * p[:, :, 1]).astype(v.dtype)
    return jnp.einsum("bhqs,bshd->bqhd", a, v)


def diff_project(h, w_qkv):
    b, t, _ = h.shape
    q, k, v = jnp.split(h @ w_qkv, 3, axis=-1)
    return (q.reshape(b, t, N_HEADS, 2, DIFF_QK_DIM),
            k.reshape(b, t, N_HEADS, 2, DIFF_QK_DIM),
            v.reshape(b, t, N_HEADS, HEAD_DIM))


def diff_lambda(lq1, lk1, lq2, lk2, lam_init):
    return (jnp.exp(jnp.sum(lq1.astype(F32) * lk1.astype(F32)))
            - jnp.exp(jnp.sum(lq2.astype(F32) * lk2.astype(F32))) + lam_init)


def diff_output(o, subln, lam_init, w_o):
    b, t = o.shape[:2]
    o = rms_norm(o, subln) * (1.0 - lam_init)
    return o.reshape(b, t, N_HEADS * HEAD_DIM) @ w_o


def diff_attention_context(h, w_qkv, w_o, lq1, lk1, lq2, lk2, subln, lam_init):
    q, k, v = diff_project(h, w_qkv)
    lam = diff_lambda(lq1, lk1, lq2, lk2, lam_init)
    o = sweep_query_blocks(lambda qb: diff_attend(qb, k, v, lam), q)
    b, l = h.shape[:2]
    new_k = swap_seq_heads(k.reshape(b, l, N_HEADS, HEAD_DIM))
    return diff_output(o, subln, lam_init, w_o), new_k, swap_seq_heads(v)


def diff_attention_latent(h, cache_k, cache_v, cos, sin, w_qkv, w_o, lq1, lk1, lq2, lk2, subln, lam_init):
    q, k, v = diff_project(h, w_qkv)
    q = apply_axial_rope(q, cos, sin)
    k = apply_axial_rope(k, cos, sin)
    b, l = cache_k.shape[0], cache_k.shape[2]
    ctx_k = swap_seq_heads(cache_k).reshape(b, l, N_HEADS, 2, DIFF_QK_DIM)
    k_all = jnp.concatenate([ctx_k, k], axis=1)
    v_all = jnp.concatenate([swap_seq_heads(cache_v), v], axis=1)
    lam = diff_lambda(lq1, lk1, lq2, lk2, lam_init)
    o = sweep_query_blocks(lambda qb: diff_attend(qb, k_all, v_all, lam), q)
    return diff_output(o, subln, lam_init, w_o)


def gqa_project(h, w_qkv, q_norm, k_norm):
    b, t, _ = h.shape
    nq = N_HEADS * HEAD_DIM
    nk = GQA_KV_HEADS * HEAD_DIM
    q, k, v = jnp.split(h @ w_qkv, [nq, nq + nk], axis=-1)
    q = rms_norm(q.reshape(b, t, N_HEADS, HEAD_DIM), q_norm)
    k = rms_norm(k.reshape(b, t, GQA_KV_HEADS, HEAD_DIM), k_norm)
    return q, k, v.reshape(b, t, GQA_KV_HEADS, HEAD_DIM)


def gqa_attention_context(h, w_qkv, w_o, q_norm, k_norm):
    q, k, v = gqa_project(h, w_qkv, q_norm, k_norm)
    o = sweep_query_blocks(lambda qb: gqa_attend(qb, k, v), q)
    b, t = h.shape[:2]
    return o.reshape(b, t, N_HEADS * HEAD_DIM) @ w_o, swap_seq_heads(k), swap_seq_heads(v)


def gqa_attention_latent(h, cache_k, cache_v, cos, sin, w_qkv, w_o, q_norm, k_norm):
    q, k, v = gqa_project(h, w_qkv, q_norm, k_norm)
    q = apply_axial_rope(q, cos, sin)
    k = apply_axial_rope(k, cos, sin)
    k_all = jnp.concatenate([swap_seq_heads(cache_k), k], axis=1)
    v_all = jnp.concatenate([swap_seq_heads(cache_v), v], axis=1)
    o = sweep_query_blocks(lambda qb: gqa_attend(qb, k_all, v_all), q)
    b, t = h.shape[:2]
    return o.reshape(b, t, N_HEADS * HEAD_DIM) @ w_o


def mha_project(h, w_qkv):
    b, t, _ = h.shape
    q, k, v = jnp.split(h @ w_qkv, 3, axis=-1)
    shape = (b, t, N_HEADS, HEAD_DIM)
    return q.reshape(shape), k.reshape(shape), v.reshape(shape)


def na_attention_context(h, w_qkv, w_o):
    q, k, v = mha_project(h, w_qkv)
    o = sweep_query_blocks(lambda qb: gqa_attend(qb, k, v), q)
    b, t = h.shape[:2]
    return o.reshape(b, t, N_HEADS * HEAD_DIM) @ w_o, swap_seq_heads(k), swap_seq_heads(v)


def na_attention_latent(h, cache_k, cache_v, w_qkv, w_o, rpb, rows):
    b, t, _ = h.shape
    q, k, v = mha_project(h, w_qkv)
    ctx_k = swap_seq_heads(cache_k)
    ctx_v = swap_seq_heads(cache_v)
    kh = min(NA_ROWS, rows)
    n_nb = kh * NA_COLS
    kg = k.reshape(b, rows, GRID_W, N_HEADS, HEAD_DIM)
    vg = v.reshape(b, rows, GRID_W, N_HEADS, HEAD_DIM)
    qg = jnp.moveaxis(q.reshape(b, rows, GRID_W, N_HEADS, HEAD_DIM), 1, 0)
    cols = jnp.arange(GRID_W)
    col_idx = jnp.clip(cols - NA_COLS // 2, 0, GRID_W - NA_COLS)[:, None] + jnp.arange(NA_COLS)
    col_bias_idx = col_idx - cols[:, None] + NA_COLS - 1
    scale = HEAD_DIM ** -0.5

    def one_row(args):
        r, q_row = args
        rs = jnp.clip(r - kh // 2, 0, rows - kh)
        k_band = lax.dynamic_slice_in_dim(kg, rs, kh, axis=1)
        v_band = lax.dynamic_slice_in_dim(vg, rs, kh, axis=1)
        k_nb = k_band[:, :, col_idx]
        v_nb = v_band[:, :, col_idx]
        row_bias_idx = rs + jnp.arange(kh) - r + NA_ROWS - 1
        bias = rpb[:, row_bias_idx[None, :, None], col_bias_idx[:, None, :]]
        s_nb = jnp.einsum("bwhd,bawchd->bhwac", q_row, k_nb, preferred_element_type=F32) * scale
        s_nb = s_nb + bias.astype(F32)
        s_ctx = jnp.einsum("bwhd,blhd->bhwl", q_row, ctx_k, preferred_element_type=F32) * scale
        s = jnp.concatenate([s_nb.reshape(b, N_HEADS, GRID_W, n_nb), s_ctx], axis=-1)
        p = jax.nn.softmax(s, axis=-1).astype(v.dtype)
        p_nb = p[..., :n_nb].reshape(b, N_HEADS, GRID_W, kh, NA_COLS)
        return (jnp.einsum("bhwac,bawchd->bwhd", p_nb, v_nb)
                + jnp.einsum("bhwl,blhd->bwhd", p[..., n_nb:], ctx_v))

    o = lax.map(one_row, (jnp.arange(rows), qg))
    o = jnp.moveaxis(o, 0, 1).reshape(b, t, N_HEADS * HEAD_DIM)
    return o @ w_o


def swiglu(h, w_gate, w_up, w_down):
    return (jax.nn.silu(h @ w_gate) * (h @ w_up)) @ w_down


def moe_swiglu(h, router, w_gate, w_up, w_down):
    b, t, d = h.shape
    n = b * t
    x = h.reshape(n, d)
    logits = jnp.einsum("nd,de->ne", x, router, preferred_element_type=F32)
    top_logit, top_idx = lax.top_k(logits, TOP_K)
    gate = jax.nn.softmax(top_logit, axis=-1).astype(x.dtype).reshape(-1)
    expert = top_idx.reshape(-1)
    token = jnp.arange(n * TOP_K) // TOP_K
    order = jnp.argsort(expert)
    expert_s, token_s, gate_s = expert[order], token[order], gate[order]
    counts = jnp.bincount(expert, length=N_EXPERTS)
    padded = (counts + MOE_BLOCK - 1) // MOE_BLOCK * MOE_BLOCK
    pad_end = jnp.cumsum(padded)
    pad_start = pad_end - padded
    raw_start = jnp.cumsum(counts) - counts
    dest = pad_start[expert_s] + jnp.arange(n * TOP_K) - raw_start[expert_s]
    n_blocks = -(-(n * TOP_K) // MOE_BLOCK) + N_EXPERTS
    buf = jnp.zeros((n_blocks * MOE_BLOCK, d), x.dtype).at[dest].set(x[token_s])
    block_expert = jnp.minimum(
        jnp.searchsorted(pad_end, jnp.arange(n_blocks) * MOE_BLOCK, side="right"), N_EXPERTS - 1)

    def expert_block(args):
        xb, e = args
        return swiglu(xb, w_gate[e], w_up[e], w_down[e])

    y = lax.map(expert_block, (buf.reshape(n_blocks, MOE_BLOCK, d), block_expert)).reshape(-1, d)
    out = jnp.zeros_like(x).at[token_s].add(y[dest] * gate_s[:, None])
    return out.reshape(b, t, d)


def setup_inputs(seed: int = 0) -> dict:
    key = jax.random.key(seed)
    keys = jax.random.split(key, 256)
    counter = [0]

    def normal(shape, scale=1.0):
        k = keys[counter[0]]
        counter[0] += 1
        return jax.random.normal(k, shape, F32) * scale

    def gain(n):
        return 1.0 + normal((n,), 0.05)

    d = D_MODEL
    inp = {
        "x_prompt": normal((BATCH, SEQ, d)),
        "x_sample": normal((DEC_BATCH, DEC_SEQ, d)),
        "c": normal((DEC_BATCH, d)),
    }
    for i in range(DEPTH):
        shape = (DEC_BATCH, kv_heads(i % N_MIXERS), PAST_LEN, HEAD_DIM)
        inp[f"cache_k_l{i}"] = normal(shape)
        inp[f"cache_v_l{i}"] = normal(shape)
    inp["c_ctx"] = normal((d,))
    inp["final_norm"] = gain(d)
    for i in range(DEPTH):
        kind = i % N_MIXERS
        inp[f"norm1_l{i}"] = gain(d)
        inp[f"norm2_l{i}"] = gain(d)
        inp[f"ada_w_l{i}"] = normal((d, 6 * d), 0.5 * d ** -0.5)
        inp[f"ada_b_l{i}"] = normal((6 * d,), 0.01)
        qkv_width = (N_HEADS + 2 * kv_heads(kind)) * HEAD_DIM
        inp[f"attn_qkv_l{i}"] = normal((d, qkv_width), d ** -0.5)
        inp[f"attn_out_l{i}"] = normal((N_HEADS * HEAD_DIM, d), (N_HEADS * HEAD_DIM) ** -0.5)
        if kind == 0:
            for name in ("lambda_q1", "lambda_k1", "lambda_q2", "lambda_k2"):
                inp[f"{name}_l{i}"] = normal((DIFF_QK_DIM,), 0.1)
            inp[f"subln_l{i}"] = gain(HEAD_DIM)
        elif kind == 1:
            inp[f"q_norm_l{i}"] = gain(HEAD_DIM)
            inp[f"k_norm_l{i}"] = gain(HEAD_DIM)
        else:
            inp[f"rpb_l{i}"] = normal((N_HEADS, 2 * NA_ROWS - 1, 2 * NA_COLS - 1), 0.2)
        if i % 2 == 0:
            inp[f"ffn_gate_l{i}"] = normal((d, D_FF), d ** -0.5)
            inp[f"ffn_up_l{i}"] = normal((d, D_FF), d ** -0.5)
            inp[f"ffn_down_l{i}"] = normal((D_FF, d), D_FF ** -0.5)
        else:
            inp[f"router_l{i}"] = normal((d, N_EXPERTS), d ** -0.5)
            inp[f"moe_gate_l{i}"] = normal((N_EXPERTS, d, D_FF), d ** -0.5)
            inp[f"moe_up_l{i}"] = normal((N_EXPERTS, d, D_FF), d ** -0.5)
            inp[f"moe_down_l{i}"] = normal((N_EXPERTS, D_FF, d), D_FF ** -0.5)
    return inp


def reference(x_prompt, x_sample, c,
              cache_k_l0, cache_v_l0, cache_k_l1, cache_v_l1,
              cache_k_l2, cache_v_l2, cache_k_l3, cache_v_l3,
              c_ctx, final_norm,
              norm1_l0, norm2_l0, ada_w_l0, ada_b_l0, attn_qkv_l0, attn_out_l0,
              lambda_q1_l0, lambda_k1_l0, lambda_q2_l0, lambda_k2_l0, subln_l0,
              ffn_gate_l0, ffn_up_l0, ffn_down_l0,
              norm1_l1, norm2_l1, ada_w_l1, ada_b_l1, attn_qkv_l1, attn_out_l1,
              q_norm_l1, k_norm_l1,
              router_l1, moe_gate_l1, moe_up_l1, moe_down_l1,
              norm1_l2, norm2_l2, ada_w_l2, ada_b_l2, attn_qkv_l2, attn_out_l2,
              rpb_l2,
              ffn_gate_l2, ffn_up_l2, ffn_down_l2,
              norm1_l3, norm2_l3, ada_w_l3, ada_b_l3, attn_qkv_l3, attn_out_l3,
              lambda_q1_l3, lambda_k1_l3, lambda_q2_l3, lambda_k2_l3, subln_l3,
              router_l3, moe_gate_l3, moe_up_l3, moe_down_l3):
    mixers = [
        (attn_qkv_l0, attn_out_l0, lambda_q1_l0, lambda_k1_l0, lambda_q2_l0, lambda_k2_l0, subln_l0),
        (attn_qkv_l1, attn_out_l1, q_norm_l1, k_norm_l1),
        (attn_qkv_l2, attn_out_l2, rpb_l2),
        (attn_qkv_l3, attn_out_l3, lambda_q1_l3, lambda_k1_l3, lambda_q2_l3, lambda_k2_l3, subln_l3),
    ]
    ffns = [
        (ffn_gate_l0, ffn_up_l0, ffn_down_l0),
        (router_l1, moe_gate_l1, moe_up_l1, moe_down_l1),
        (ffn_gate_l2, ffn_up_l2, ffn_down_l2),
        (router_l3, moe_gate_l3, moe_up_l3, moe_down_l3),
    ]
    caches = [(cache_k_l0, cache_v_l0), (cache_k_l1, cache_v_l1),
              (cache_k_l2, cache_v_l2), (cache_k_l3, cache_v_l3)]
    norms = [(norm1_l0, norm2_l0), (norm1_l1, norm2_l1), (norm1_l2, norm2_l2), (norm1_l3, norm2_l3)]
    adas = [(ada_w_l0, ada_b_l0), (ada_w_l1, ada_b_l1), (ada_w_l2, ada_b_l2), (ada_w_l3, ada_b_l3)]

    t_lat = x_sample.shape[1]
    rows = t_lat // GRID_W
    cos_d, sin_d = axial_rope_tables(t_lat, DIFF_QK_DIM, x_sample.dtype)
    cos_g, sin_g = axial_rope_tables(t_lat, HEAD_DIM, x_sample.dtype)

    xp, xs = x_prompt, x_sample
    new_state = []
    for i in range(DEPTH):
        kind = i % N_MIXERS
        gain1, gain2 = norms[i]
        sh1p, sc1p, g1p, sh2p, sc2p, g2p = adaln_params(c_ctx[None, :], *adas[i])
        sh1s, sc1s, g1s, sh2s, sc2s, g2s = adaln_params(c, *adas[i])
        hp = modulate(xp, gain1, sh1p, sc1p)
        hs = modulate(xs, gain1, sh1s, sc1s)
        ck, cv = caches[i]
        if kind == 0:
            lam_init = 0.8 - 0.6 * math.exp(-0.3 * i)
            yp, kp, vp = diff_attention_context(hp, *mixers[i], lam_init)
            ys = diff_attention_latent(hs, ck, cv, cos_d, sin_d, *mixers[i], lam_init)
        elif kind == 1:
            yp, kp, vp = gqa_attention_context(hp, *mixers[i])
            ys = gqa_attention_latent(hs, ck, cv, cos_g, sin_g, *mixers[i])
        else:
            yp, kp, vp = na_attention_context(hp, *mixers[i][:2])
            ys = na_attention_latent(hs, ck, cv, *mixers[i], rows)
        new_state += [kp, vp]
        xp = xp + g1p * yp
        xs = xs + g1s * ys
        channel = swiglu if i % 2 == 0 else moe_swiglu
        xp = xp + g2p * channel(modulate(xp, gain2, sh2p, sc2p), *ffns[i])
        xs = xs + g2s * channel(modulate(xs, gain2, sh2s, sc2s), *ffns[i])

    y_prompt = rms_norm(xp, final_norm)
    y_sample = rms_norm(xs, final_norm)
    return (y_prompt, y_sample, *new_state)
```

```python
import functools
import math

import numpy as np
import jax
import jax.numpy as jnp
from jax import lax
from jax.experimental import pallas as pl
from jax.experimental.pallas import tpu as pltpu

F32 = jnp.float32
BF16 = jnp.bfloat16

HEAD_DIM = 128
GRID_W = 64
NA_ROWS = 8
NA_COLS = 16
NA_QROWS = 4
NA_BAND = 12
TOP_K = 2
ROPE_THETA = 10000.0
NORM_EPS = 1e-6
MASK_VALUE = -1e30
VMEM_LIMIT_BYTES = 56 * 2**20
MOE_TILE = 256


def _cparams(*sem):
    return pltpu.CompilerParams(dimension_semantics=sem, vmem_limit_bytes=VMEM_LIMIT_BYTES)


def _pick(n, pref, mult=128):
    t = min(pref, n)
    t -= t % mult
    while t > mult and n % t:
        t -= mult
    assert t >= mult and n % t == 0, (n, pref)
    return t


def _dot(a, b):
    return jnp.dot(a, b, preferred_element_type=F32)


def _dot_nt(a, b):
    return lax.dot_general(a, b, (((1,), (1,)), ((), ())), preferred_element_type=F32)


def _rms(x):
    return x * lax.rsqrt(jnp.mean(x * x, axis=-1, keepdims=True) + NORM_EPS)


def _modulate(x, gain, shift, scale):
    return _rms(x) * gain * (1.0 + scale) + shift


def _group_of_tile(i, tm, n_ctx, t_lat):
    row = i * tm
    return jnp.where(row < n_ctx, 0, 1 + (row - n_ctx) // t_lat)


def _adaln_kernel(c_ref, w_ref, b_ref, o_ref):
    c = c_ref[...]
    s = c * jax.nn.sigmoid(c)
    o_ref[...] = _dot(s.astype(BF16), w_ref[...].astype(BF16)) + b_ref[...]


def adaln(cond8, w, b):
    d, n = w.shape
    tn = _pick(n, 1024)
    return pl.pallas_call(
        _adaln_kernel,
        grid=(n // tn,),
        in_specs=[pl.BlockSpec((8, d), lambda j: (0, 0)),
                  pl.BlockSpec((d, tn), lambda j: (0, j)),
                  pl.BlockSpec((1, tn), lambda j: (0, j))],
        out_specs=pl.BlockSpec((8, tn), lambda j: (0, j)),
        out_shape=jax.ShapeDtypeStruct((8, n), F32),
        compiler_params=_cparams("parallel"),
        name="adaln",
    )(cond8, w, b.reshape(1, n))


def _nmm_kernel(x_ref, g_ref, sh_ref, sc_ref, w_ref, o_ref, h_ref):
    @pl.when(pl.program_id(1) == 0)
    def _():
        h_ref[...] = _modulate(x_ref[...], g_ref[...], sh_ref[0], sc_ref[0]).astype(BF16)

    o_ref[...] = _dot(h_ref[...], w_ref[...].astype(BF16)).astype(o_ref.dtype)


def _nmm_swiglu_kernel(x_ref, g_ref, sh_ref, sc_ref, wg_ref, wu_ref, o_ref, h_ref):
    @pl.when(pl.program_id(1) == 0)
    def _():
        h_ref[...] = _modulate(x_ref[...], g_ref[...], sh_ref[0], sc_ref[0]).astype(BF16)

    h = h_ref[...]
    a = _dot(h, wg_ref[...].astype(BF16))
    b = _dot(h, wu_ref[...].astype(BF16))
    o_ref[...] = (a * jax.nn.sigmoid(a) * b).astype(o_ref.dtype)


def norm_mod_matmul(x, gain, shift, scale, ws, out_dtype, n_ctx, t_lat, tm=512, tn=512):
    m, d = x.shape
    n = ws[0].shape[1]
    tm = _pick(math.gcd(n_ctx, t_lat), tm)
    tn = _pick(n, tn)
    grp = lambda i, j: (_group_of_tile(i, tm, n_ctx, t_lat), 0, 0)
    kern = _nmm_kernel if len(ws) == 1 else _nmm_swiglu_kernel
    return pl.pallas_call(
        kern,
        grid=(m // tm, n // tn),
        in_specs=[pl.BlockSpec((tm, d), lambda i, j: (i, 0)),
                  pl.BlockSpec((1, d), lambda i, j: (0, 0)),
                  pl.BlockSpec((1, 1, d), grp),
                  pl.BlockSpec((1, 1, d), grp)]
                 + [pl.BlockSpec((d, tn), lambda i, j: (0, j)) for _ in ws],
        out_specs=pl.BlockSpec((tm, tn), lambda i, j: (i, j)),
        out_shape=jax.ShapeDtypeStruct((m, n), out_dtype),
        scratch_shapes=[pltpu.VMEM((tm, d), BF16)],
        compiler_params=_cparams("parallel", "arbitrary"),
        name="norm_mod_matmul" if len(ws) == 1 else "norm_mod_swiglu",
    )(x, gain.reshape(1, d), shift, scale, *ws)


def _mmres_kernel(a_ref, w_ref, r_ref, g_ref, o_ref, acc_ref, *, nk):
    k = pl.program_id(2)

    @pl.when(k == 0)
    def _():
        acc_ref[...] = jnp.zeros_like(acc_ref)

    acc_ref[...] += _dot(a_ref[...], w_ref[...].astype(BF16))

    @pl.when(k == nk - 1)
    def _():
        o_ref[...] = r_ref[...] + g_ref[0] * acc_ref[...]


def matmul_residual(a, w, res, gate, n_ctx, t_lat, tm=512, tn=1024, tk=1024):
    m, kdim = a.shape
    n = w.shape[1]
    tm = _pick(math.gcd(n_ctx, t_lat), tm)
    tn = _pick(n, tn)
    tk = _pick(kdim, tk)
    nk = kdim // tk
    return pl.pallas_call(
        functools.partial(_mmres_kernel, nk=nk),
        grid=(m // tm, n // tn, nk),
        in_specs=[pl.BlockSpec((tm, tk), lambda i, j, k: (i, k)),
                  pl.BlockSpec((tk, tn), lambda i, j, k: (k, j)),
                  pl.BlockSpec((tm, tn), lambda i, j, k: (i, j)),
                  pl.BlockSpec((1, 1, tn), lambda i, j, k: (_group_of_tile(i, tm, n_ctx, t_lat), 0, j))],
        out_specs=pl.BlockSpec((tm, tn), lambda i, j, k: (i, j)),
        out_shape=jax.ShapeDtypeStruct((m, n), F32),
        scratch_shapes=[pltpu.VMEM((tm, tn), F32)],
        compiler_params=_cparams("parallel", "parallel", "arbitrary"),
        name="matmul_residual",
    )(a, w, res, gate)


def _softmax_rows(s):
    p = jnp.exp(s - jnp.max(s, axis=-1, keepdims=True))
    return p / jnp.sum(p, axis=-1, keepdims=True)


def _diff_lambda(lq1_ref, lk1_ref, lq2_ref, lk2_ref, lam_init):
    a = jnp.sum(lq1_ref[...] * lk1_ref[...], axis=-1, keepdims=True)
    b = jnp.sum(lq2_ref[...] * lk2_ref[...], axis=-1, keepdims=True)
    return jnp.exp(a) - jnp.exp(b) + lam_init


def _diff_attend(q, k_bf, v_bf, lam, subln, lam_init):
    half = HEAD_DIM // 2
    first = lax.broadcasted_iota(jnp.int32, (1, HEAD_DIM), 1) < half
    scale = half ** -0.5
    q1 = jnp.where(first, q, 0.0).astype(BF16)
    q2 = jnp.where(first, 0.0, q).astype(BF16)
    p1 = _softmax_rows(_dot_nt(q1, k_bf) * scale)
    p2 = _softmax_rows(_dot_nt(q2, k_bf) * scale)
    o = _dot((p1 - lam * p2).astype(BF16), v_bf)
    return _rms(o) * subln * (1.0 - lam_init)


def _plain_attend(q_bf, k_bf, v_bf):
    s = _dot_nt(q_bf, k_bf) * (HEAD_DIM ** -0.5)
    p = jnp.exp(s - jnp.max(s, axis=-1, keepdims=True))
    return _dot(p.astype(BF16), v_bf) / jnp.sum(p, axis=-1, keepdims=True)


def _rope(x, cos, sin, quarter):
    lane = lax.broadcasted_iota(jnp.int32, (1, HEAD_DIM), 1)
    even = (lane // quarter) % 2 == 0
    nxt = pltpu.roll(x, HEAD_DIM - quarter, 1)
    prv = pltpu.roll(x, quarter, 1)
    return x * cos + jnp.where(even, -nxt, prv) * sin


def _rope_tables(t_lat, dim):
    quarter = dim // 4
    inv_freq = ROPE_THETA ** (-jnp.arange(quarter, dtype=F32) / quarter)
    t = jnp.arange(t_lat)
    ang_r = (t // GRID_W).astype(F32)[:, None] * inv_freq
    ang_c = (t % GRID_W).astype(F32)[:, None] * inv_freq
    ang = jnp.concatenate([ang_r, ang_r, ang_c, ang_c] * (HEAD_DIM // dim), axis=-1)
    return jnp.cos(ang), jnp.sin(ang)


def _ctx_attn_kernel(*refs, mode, lam_init):
    q_ref, k_ref, v_ref = refs[:3]
    o_ref, nk_ref, nv_ref = refs[-3:]
    q, k, v = q_ref[...], k_ref[...], v_ref[...]
    if mode == "gqa":
        qn_ref, kn_ref = refs[3:5]
        q = _rms(q) * qn_ref[...]
        k = _rms(k) * kn_ref[...]
    nk_ref[...] = k
    nv_ref[...] = v
    if mode == "diff":
        lam = _diff_lambda(*refs[3:7], lam_init)
        o = _diff_attend(q, k.astype(BF16), v.astype(BF16), lam, refs[7][...], lam_init)
    else:
        o = _plain_attend(q.astype(BF16), k.astype(BF16), v.astype(BF16))
    o_ref[...] = o.astype(o_ref.dtype)


def ctx_attention(qkv, params, mode, lam_init, n_batch, seq, n_heads, n_kv, m_total):
    grp = n_heads // n_kv
    blk = lambda f: pl.BlockSpec((seq, HEAD_DIM), f)
    small = [pl.BlockSpec((1, p.shape[-1]), lambda b, h: (0, 0)) for p in params]
    state = pl.BlockSpec((None, None, seq, HEAD_DIM), lambda b, h: (b, h // grp, 0, 0))
    st_shape = jax.ShapeDtypeStruct((n_batch, n_kv, seq, HEAD_DIM), F32)
    return pl.pallas_call(
        functools.partial(_ctx_attn_kernel, mode=mode, lam_init=lam_init),
        grid=(n_batch, n_heads),
        in_specs=[blk(lambda b, h: (b, h)),
                  blk(lambda b, h: (b, n_heads + h // grp)),
                  blk(lambda b, h: (b, n_heads + n_kv + h // grp))] + small,
        out_specs=[blk(lambda b, h: (b, h)), state, state],
        out_shape=[jax.ShapeDtypeStruct((m_total, n_heads * HEAD_DIM), BF16), st_shape, st_shape],
        compiler_params=_cparams("parallel", "arbitrary"),
        name="ctx_attention_" + mode,
    )(qkv, qkv, qkv, *[p.reshape(1, -1) for p in params])


def _lat_attn_kernel(*refs, mode, lam_init, n_ctx_keys, grp, quarter):
    (o_in_ref, q_ref, kl_ref, vl_ref, ck_ref, cv_ref,
     cosq_ref, sinq_ref, cosk_ref, sink_ref) = refs[:10]
    o_ref, kall_ref, vall_ref = refs[-3:]
    del o_in_ref
    params = refs[10:-3]
    h, qi = pl.program_id(1), pl.program_id(2)

    @pl.when(jnp.logical_and(qi == 0, h % grp == 0))
    def _():
        k = kl_ref[...]
        if mode == "gqa":
            k = _rms(k) * params[1][...]
        kall_ref[:n_ctx_keys, :] = ck_ref[...].astype(BF16)
        kall_ref[n_ctx_keys:, :] = _rope(k, cosk_ref[...], sink_ref[...], quarter).astype(BF16)
        vall_ref[:n_ctx_keys, :] = cv_ref[...].astype(BF16)
        vall_ref[n_ctx_keys:, :] = vl_ref[...].astype(BF16)

    q = q_ref[...]
    if mode == "gqa":
        q = _rms(q) * params[0][...]
    q = _rope(q, cosq_ref[...], sinq_ref[...], quarter)
    if mode == "diff":
        lam = _diff_lambda(*params[:4], lam_init)
        o = _diff_attend(q, kall_ref[...], vall_ref[...], lam, params[4][...], lam_init)
    else:
        o = _plain_attend(q.astype(BF16), kall_ref[...], vall_ref[...])
    o_ref[...] = o.astype(o_ref.dtype)


def latent_attention(o_ctx, qkv, cache_k, cache_v, params, mode, lam_init,
                     n_ctx, n_batch, t_lat, n_heads, n_kv, tq=256):
    grp = n_heads // n_kv
    n_past = cache_k.shape[2]
    tq = _pick(math.gcd(n_ctx, t_lat), tq)
    nq = t_lat // tq
    assert n_ctx % t_lat == 0
    base_q, base_k = n_ctx // tq, n_ctx // t_lat
    dim = HEAD_DIM // 2 if mode == "diff" else HEAD_DIM
    cos, sin = _rope_tables(t_lat, dim)
    qblk = pl.BlockSpec((tq, HEAD_DIM), lambda b, h, i: (base_q + b * nq + i, h))
    kvblk = lambda off: pl.BlockSpec((t_lat, HEAD_DIM), lambda b, h, i: (base_k + b, off + h // grp))
    cblk = pl.BlockSpec((None, None, n_past, HEAD_DIM), lambda b, h, i: (b, h // grp, 0, 0))
    tabq = pl.BlockSpec((tq, HEAD_DIM), lambda b, h, i: (i, 0))
    tabk = pl.BlockSpec((t_lat, HEAD_DIM), lambda b, h, i: (0, 0))
    small = [pl.BlockSpec((1, p.shape[-1]), lambda b, h, i: (0, 0)) for p in params]
    return pl.pallas_call(
        functools.partial(_lat_attn_kernel, mode=mode, lam_init=lam_init,
                          n_ctx_keys=n_past, grp=grp, quarter=dim // 4),
        grid=(n_batch, n_heads, nq),
        in_specs=[pl.BlockSpec(memory_space=pl.ANY), qblk, kvblk(n_heads), kvblk(n_heads + n_kv),
                  cblk, cblk, tabq, tabq, tabk, tabk] + small,
        out_specs=qblk,
        out_shape=jax.ShapeDtypeStruct(o_ctx.shape, o_ctx.dtype),
        scratch_shapes=[pltpu.VMEM((n_past + t_lat, HEAD_DIM), BF16),
                        pltpu.VMEM((n_past + t_lat, HEAD_DIM), BF16)],
        input_output_aliases={0: 0},
        compiler_params=_cparams("parallel", "arbitrary", "arbitrary"),
        name="latent_attention_" + mode,
    )(o_ctx, qkv, qkv, qkv, cache_k, cache_v, cos, sin, cos, sin,
      *[p.reshape(1, -1) for p in params])


def _na_geometry(rows):
    kh = min(NA_ROWS, rows)
    band = min(NA_BAND, rows)
    nblk = rows // NA_QROWS
    starts, pat_ids, pats, keys = [], [], [], []
    qr = np.repeat(np.arange(NA_QROWS), GRID_W)[:, None]
    qc = np.tile(np.arange(GRID_W), NA_QROWS)[:, None]
    kr = np.repeat(np.arange(band), GRID_W)[None, :]
    kc = np.tile(np.arange(GRID_W), band)[None, :]
    cs = np.clip(qc - NA_COLS // 2, 0, GRID_W - NA_COLS)
    for j in range(nblk):
        r0 = j * NA_QROWS
        bs = int(np.clip(r0 - kh // 2, 0, rows - band))
        r = r0 + qr
        rs = np.clip(r - kh // 2, 0, rows - kh)
        key_row = bs + kr
        valid = (key_row >= rs) & (key_row < rs + kh) & (kc >= cs) & (kc < cs + NA_COLS)
        ri = np.where(valid, key_row - r + NA_ROWS - 1, 0)
        ci = np.where(valid, kc - qc + NA_COLS - 1, 0)
        key = (bs - r0, tuple((rs - r0).ravel()))
        if key not in keys:
            keys.append(key)
            pats.append((valid, ri, ci))
        starts.append(bs)
        pat_ids.append(keys.index(key))
    return starts, pat_ids, pats


def _na_attn_kernel(start_ref, pat_ref, o_in_ref, q_ref, kl_ref, vl_ref, ck_ref, cv_ref, bias_ref,
                    o_ref, *, band_keys):
    del pat_ref, o_in_ref
    start = pl.multiple_of(start_ref[pl.program_id(2)] * GRID_W, GRID_W)
    scale = HEAD_DIM ** -0.5
    q = q_ref[...].astype(BF16)
    kb = kl_ref[pl.ds(start, band_keys), :].astype(BF16)
    vb = vl_ref[pl.ds(start, band_keys), :].astype(BF16)
    s_nb = _dot_nt(q, kb) * scale + bias_ref[...]
    s_cx = _dot_nt(q, ck_ref[...].astype(BF16)) * scale
    m = jnp.maximum(jnp.max(s_nb, axis=-1, keepdims=True), jnp.max(s_cx, axis=-1, keepdims=True))
    p_nb = jnp.exp(s_nb - m)
    p_cx = jnp.exp(s_cx - m)
    l = jnp.sum(p_nb, axis=-1, keepdims=True) + jnp.sum(p_cx, axis=-1, keepdims=True)
    o = _dot(p_nb.astype(BF16), vb) + _dot(p_cx.astype(BF16), cv_ref[...].astype(BF16))
    o_ref[...] = (o / l).astype(o_ref.dtype)


def na_latent_attention(o_ctx, qkv, cache_k, cache_v, rpb, n_ctx, n_batch, t_lat, n_heads):
    rows = t_lat // GRID_W
    assert rows % NA_QROWS == 0 and n_ctx % t_lat == 0
    starts, pat_ids, pats = _na_geometry(rows)
    nblk = len(starts)
    tq = NA_QROWS * GRID_W
    band_keys = min(NA_BAND, rows) * GRID_W
    n_past = cache_k.shape[2]
    bias = jnp.stack([jnp.where(v[None], rpb[:, ri, ci], MASK_VALUE) for v, ri, ci in pats])
    base_q, base_k = n_ctx // tq, n_ctx // t_lat
    qblk = pl.BlockSpec((tq, HEAD_DIM), lambda b, h, j, st, pt: (base_q + b * nblk + j, h))
    kvblk = lambda off: pl.BlockSpec((t_lat, HEAD_DIM), lambda b, h, j, st, pt: (base_k + b, off + h))
    cblk = pl.BlockSpec((None, None, n_past, HEAD_DIM), lambda b, h, j, st, pt: (b, h, 0, 0))
    bblk = pl.BlockSpec((None, None, tq, band_keys), lambda b, h, j, st, pt: (pt[j], h, 0, 0))
    return pl.pallas_call(
        functools.partial(_na_attn_kernel, band_keys=band_keys),
        grid_spec=pltpu.PrefetchScalarGridSpec(
            num_scalar_prefetch=2,
            grid=(n_batch, n_heads, nblk),
            in_specs=[pl.BlockSpec(memory_space=pl.ANY), qblk, kvblk(n_heads), kvblk(2 * n_heads),
                      cblk, cblk, bblk],
            out_specs=qblk),
        out_shape=jax.ShapeDtypeStruct(o_ctx.shape, o_ctx.dtype),
        input_output_aliases={2: 0},
        compiler_params=_cparams("parallel", "arbitrary", "arbitrary"),
        name="latent_attention_na",
    )(jnp.asarray(starts, jnp.int32), jnp.asarray(pat_ids, jnp.int32),
      o_ctx, qkv, qkv, qkv, cache_k, cache_v, bias)


def _router_kernel(x_ref, g_ref, sh_ref, sc_ref, rt_ref, h_ref, idx_ref, gate_ref):
    h = _modulate(x_ref[...], g_ref[...], sh_ref[0], sc_ref[0])
    h_ref[...] = h
    logits = lax.dot_general(rt_ref[...], h, (((1,), (1,)), ((), ())),
                             precision=lax.Precision.HIGHEST, preferred_element_type=F32)
    n_exp = logits.shape[0]
    eid = lax.broadcasted_iota(jnp.int32, logits.shape, 0)
    m1 = jnp.max(logits, axis=0, keepdims=True)
    i1 = jnp.min(jnp.where(logits == m1, eid, n_exp), axis=0, keepdims=True)
    rest = jnp.where(eid == i1, -jnp.inf, logits)
    m2 = jnp.max(rest, axis=0, keepdims=True)
    i2 = jnp.min(jnp.where(rest == m2, eid, n_exp), axis=0, keepdims=True)
    e = jnp.exp(m2 - m1)
    idx_ref[0:1, :] = i1
    idx_ref[1:2, :] = i2
    gate_ref[0:1, :] = 1.0 / (1.0 + e)
    gate_ref[1:2, :] = e / (1.0 + e)


def moe_route(x, gain, shift, scale, router, n_ctx, t_lat, tm=512):
    m, d = x.shape
    n_exp = router.shape[1]
    tm = _pick(math.gcd(n_ctx, t_lat), tm)
    grp = lambda i: (_group_of_tile(i, tm, n_ctx, t_lat), 0, 0)
    return pl.pallas_call(
        _router_kernel,
        grid=(m // tm,),
        in_specs=[pl.BlockSpec((tm, d), lambda i: (i, 0)),
                  pl.BlockSpec((1, d), lambda i: (0, 0)),
                  pl.BlockSpec((1, 1, d), grp),
                  pl.BlockSpec((1, 1, d), grp),
                  pl.BlockSpec((n_exp, d), lambda i: (0, 0))],
        out_specs=[pl.BlockSpec((tm, d), lambda i: (i, 0)),
                   pl.BlockSpec((TOP_K, tm), lambda i: (0, i)),
                   pl.BlockSpec((TOP_K, tm), lambda i: (0, i))],
        out_shape=[jax.ShapeDtypeStruct((m, d), F32),
                   jax.ShapeDtypeStruct((TOP_K, m), jnp.int32),
                   jax.ShapeDtypeStruct((TOP_K, m), F32)],
        compiler_params=_cparams("parallel"),
        name="moe_router",
    )(x, gain.reshape(1, d), shift, scale, router.T)


def _gather_rows_kernel(src_ref, h_hbm, o_ref, buf_ref, sem):
    t = pl.program_id(0)
    rows = buf_ref.shape[0]

    def row_copy(r):
        return pltpu.make_async_copy(h_hbm.at[pl.ds(src_ref[t * rows + r], 1)],
                                     buf_ref.at[pl.ds(r, 1)], sem)

    def start(r, c):
        row_copy(r).start()
        return c

    def wait(r, c):
        row_copy(r).wait()
        return c

    lax.fori_loop(0, rows, start, 0)
    lax.fori_loop(0, rows, wait, 0)
    o_ref[...] = buf_ref[...].astype(o_ref.dtype)


def gather_rows(h, src, n_tiles):
    d = h.shape[1]
    return pl.pallas_call(
        _gather_rows_kernel,
        grid_spec=pltpu.PrefetchScalarGridSpec(
            num_scalar_prefetch=1,
            grid=(n_tiles,),
            in_specs=[pl.BlockSpec(memory_space=pl.ANY)],
            out_specs=pl.BlockSpec((MOE_TILE, d), lambda t, s: (t, 0)),
            scratch_shapes=[pltpu.VMEM((MOE_TILE, d), h.dtype), pltpu.SemaphoreType.DMA(())]),
        out_shape=jax.ShapeDtypeStruct((n_tiles * MOE_TILE, d), BF16),
        compiler_params=_cparams("arbitrary"),
        name="moe_gather",
    )(src, h)


def _expert_up_kernel(te_ref, nu_ref, x_ref, wg_ref, wu_ref, o_ref):
    @pl.when(pl.program_id(1) < nu_ref[0])
    def _():
        x = x_ref[...]
        a = _dot(x, wg_ref[...].astype(BF16))
        b = _dot(x, wu_ref[...].astype(BF16))
        o_ref[...] = (a * jax.nn.sigmoid(a) * b).astype(o_ref.dtype)

    @pl.when(pl.program_id(1) >= nu_ref[0])
    def _():
        o_ref[...] = jnp.zeros_like(o_ref)


def expert_up(xs, w_gate, w_up, tile_expert, n_used, tf=512):
    p, d = xs.shape
    f = w_gate.shape[2]
    tf = _pick(f, tf)
    nt = p // MOE_TILE
    used = lambda t, nu: jnp.minimum(t, nu[0] - 1)
    wblk = pl.BlockSpec((None, d, tf), lambda j, t, te, nu: (te[used(t, nu)], 0, j))
    return pl.pallas_call(
        _expert_up_kernel,
        grid_spec=pltpu.PrefetchScalarGridSpec(
            num_scalar_prefetch=2,
            grid=(f // tf, nt),
            in_specs=[pl.BlockSpec((MOE_TILE, d), lambda j, t, te, nu: (used(t, nu), 0)), wblk, wblk],
            out_specs=pl.BlockSpec((MOE_TILE, tf), lambda j, t, te, nu: (t, j))),
        out_shape=jax.ShapeDtypeStruct((p, f), BF16),
        compiler_params=_cparams("parallel", "arbitrary"),
        name="moe_expert_up",
    )(tile_expert, n_used, xs, w_gate, w_up)


def _expert_down_kernel(te_ref, nu_ref, u_ref, w_ref, g_ref, o_ref, *, tk):
    @pl.when(pl.program_id(1) < nu_ref[0])
    def _():
        acc = jnp.zeros(o_ref.shape, F32)
        for k0 in range(0, u_ref.shape[1], tk):
            acc += _dot(u_ref[:, k0:k0 + tk], w_ref[k0:k0 + tk, :].astype(BF16))
        o_ref[...] = acc * g_ref[...]

    @pl.when(pl.program_id(1) >= nu_ref[0])
    def _():
        o_ref[...] = jnp.zeros_like(o_ref)


def expert_down(u, w_down, gate_rows, tile_expert, n_used, tn=512, tk=1024):
    p, f = u.shape
    d = w_down.shape[2]
    tn = _pick(d, tn)
    tk = _pick(f, tk)
    nt = p // MOE_TILE
    used = lambda t, nu: jnp.minimum(t, nu[0] - 1)
    return pl.pallas_call(
        functools.partial(_expert_down_kernel, tk=tk),
        grid_spec=pltpu.PrefetchScalarGridSpec(
            num_scalar_prefetch=2,
            grid=(d // tn, nt),
            in_specs=[pl.BlockSpec((MOE_TILE, f), lambda j, t, te, nu: (used(t, nu), 0)),
                      pl.BlockSpec((None, f, tn), lambda j, t, te, nu: (te[used(t, nu)], 0, j)),
                      pl.BlockSpec((MOE_TILE, 1), lambda j, t, te, nu: (used(t, nu), 0))],
            out_specs=pl.BlockSpec((MOE_TILE, tn), lambda j, t, te, nu: (t, j))),
        out_shape=jax.ShapeDtypeStruct((p, d), F32),
        compiler_params=_cparams("parallel", "arbitrary"),
        name="moe_expert_down",
    )(tile_expert, n_used, u, w_down, gate_rows)


def _combine_kernel(p0_ref, p1_ref, x_ref, g_ref, y_hbm, o_ref, ya_ref, yb_ref, sem):
    i = pl.program_id(0)
    rows = ya_ref.shape[0]

    def copies(r):
        return (pltpu.make_async_copy(y_hbm.at[pl.ds(p0_ref[i * rows + r], 1)],
                                      ya_ref.at[pl.ds(r, 1)], sem.at[0]),
                pltpu.make_async_copy(y_hbm.at[pl.ds(p1_ref[i * rows + r], 1)],
                                      yb_ref.at[pl.ds(r, 1)], sem.at[1]))

    def start(r, c):
        for cp in copies(r):
            cp.start()
        return c

    def wait(r, c):
        for cp in copies(r):
            cp.wait()
        return c

    lax.fori_loop(0, rows, start, 0)
    lax.fori_loop(0, rows, wait, 0)
    o_ref[...] = x_ref[...] + g_ref[0] * (ya_ref[...] + yb_ref[...])


def moe_combine(x, gate, y, pos0, pos1, n_ctx, t_lat, tm=256):
    m, d = x.shape
    tm = _pick(math.gcd(n_ctx, t_lat), tm)
    return pl.pallas_call(
        _combine_kernel,
        grid_spec=pltpu.PrefetchScalarGridSpec(
            num_scalar_prefetch=2,
            grid=(m // tm,),
            in_specs=[pl.BlockSpec((tm, d), lambda i, a, b: (i, 0)),
                      pl.BlockSpec((1, 1, d), lambda i, a, b: (_group_of_tile(i, tm, n_ctx, t_lat), 0, 0)),
                      pl.BlockSpec(memory_space=pl.ANY)],
            out_specs=pl.BlockSpec((tm, d), lambda i, a, b: (i, 0)),
            scratch_shapes=[pltpu.VMEM((tm, d), F32), pltpu.VMEM((tm, d), F32),
                            pltpu.SemaphoreType.DMA((2,))]),
        out_shape=jax.ShapeDtypeStruct((m, d), F32),
        compiler_params=_cparams("arbitrary"),
        name="moe_combine",
    )(pos0, pos1, x, gate, y)


def moe_layer(x, gain, shift, scale, gate, router, w_gate, w_up, w_down, n_ctx, t_lat):
    m, _ = x.shape
    n_exp = router.shape[1]
    h, idx, gates = moe_route(x, gain, shift, scale, router, n_ctx, t_lat)
    expert = idx.T.reshape(-1)
    n_assign = expert.shape[0]
    onehot = (expert[:, None] == jnp.arange(n_exp)[None, :]).astype(jnp.int32)
    csum = jnp.cumsum(onehot, axis=0)
    rank = jnp.sum(onehot * (csum - onehot), axis=1)
    counts = csum[-1]
    padded = (counts + MOE_TILE - 1) // MOE_TILE * MOE_TILE
    pad_end = jnp.cumsum(padded)
    dest = (pad_end - padded)[expert] + rank
    n_tiles = n_assign // MOE_TILE + n_exp
    token = jnp.arange(n_assign, dtype=jnp.int32) // TOP_K
    src = jnp.zeros((n_tiles * MOE_TILE,), jnp.int32).at[dest].set(token)
    gate_rows = jnp.zeros((n_tiles * MOE_TILE, 1), F32).at[dest, 0].set(gates.T.reshape(-1))
    tile_expert = jnp.minimum(
        jnp.searchsorted(pad_end, jnp.arange(n_tiles) * MOE_TILE, side="right"), n_exp - 1).astype(jnp.int32)
    n_used = (pad_end[-1:] // MOE_TILE).astype(jnp.int32)
    pos = dest.reshape(m, TOP_K).astype(jnp.int32)

    xs = gather_rows(h, src, n_tiles)
    u = expert_up(xs, w_gate, w_up, tile_expert, n_used)
    y = expert_down(u, w_down, gate_rows, tile_expert, n_used)
    return moe_combine(x, gate, y, pos[:, 0], pos[:, 1], n_ctx, t_lat)


def _final_norm_kernel(x_ref, g_ref, o_ref):
    o_ref[...] = _rms(x_ref[...]) * g_ref[...]


def final_rmsnorm(x, gain, tm=512):
    m, d = x.shape
    tm = _pick(m, tm)
    return pl.pallas_call(
        _final_norm_kernel,
        grid=(m // tm,),
        in_specs=[pl.BlockSpec((tm, d), lambda i: (i, 0)), pl.BlockSpec((1, d), lambda i: (0, 0))],
        out_specs=pl.BlockSpec((tm, d), lambda i: (i, 0)),
        out_shape=jax.ShapeDtypeStruct((m, d), F32),
        compiler_params=_cparams("parallel"),
        name="final_norm",
    )(x, gain.reshape(1, d))


def kernel(x_prompt, x_sample, c, cache_k_l0, cache_v_l0, cache_k_l1, cache_v_l1, cache_k_l2, cache_v_l2, cache_k_l3, cache_v_l3, c_ctx, final_norm, norm1_l0, norm2_l0, ada_w_l0, ada_b_l0, attn_qkv_l0, attn_out_l0, lambda_q1_l0, lambda_k1_l0, lambda_q2_l0, lambda_k2_l0, subln_l0, ffn_gate_l0, ffn_up_l0, ffn_down_l0, norm1_l1, norm2_l1, ada_w_l1, ada_b_l1, attn_qkv_l1, attn_out_l1, q_norm_l1, k_norm_l1, router_l1, moe_gate_l1, moe_up_l1, moe_down_l1, norm1_l2, norm2_l2, ada_w_l2, ada_b_l2, attn_qkv_l2, attn_out_l2, rpb_l2, ffn_gate_l2, ffn_up_l2, ffn_down_l2, norm1_l3, norm2_l3, ada_w_l3, ada_b_l3, attn_qkv_l3, attn_out_l3, lambda_q1_l3, lambda_k1_l3, lambda_q2_l3, lambda_k2_l3, subln_l3, router_l3, moe_gate_l3, moe_up_l3, moe_down_l3):
    final_gain = final_norm
    mixers = [
        ("diff", attn_qkv_l0, attn_out_l0, (lambda_q1_l0, lambda_k1_l0, lambda_q2_l0, lambda_k2_l0, subln_l0)),
        ("gqa", attn_qkv_l1, attn_out_l1, (q_norm_l1, k_norm_l1)),
        ("na", attn_qkv_l2, attn_out_l2, (rpb_l2,)),
        ("diff", attn_qkv_l3, attn_out_l3, (lambda_q1_l3, lambda_k1_l3, lambda_q2_l3, lambda_k2_l3, subln_l3)),
    ]
    ffns = [
        (ffn_gate_l0, ffn_up_l0, ffn_down_l0),
        (router_l1, moe_gate_l1, moe_up_l1, moe_down_l1),
        (ffn_gate_l2, ffn_up_l2, ffn_down_l2),
        (router_l3, moe_gate_l3, moe_up_l3, moe_down_l3),
    ]
    caches = [(cache_k_l0, cache_v_l0), (cache_k_l1, cache_v_l1),
              (cache_k_l2, cache_v_l2), (cache_k_l3, cache_v_l3)]
    norms = [(norm1_l0, norm2_l0), (norm1_l1, norm2_l1), (norm1_l2, norm2_l2), (norm1_l3, norm2_l3)]
    adas = [(ada_w_l0, ada_b_l0), (ada_w_l1, ada_b_l1), (ada_w_l2, ada_b_l2), (ada_w_l3, ada_b_l3)]

    n_batch, seq, d = x_prompt.shape
    bs, t_lat, _ = x_sample.shape
    n_ctx = n_batch * seq
    m = n_ctx + bs * t_lat
    n_heads = d // HEAD_DIM
    n_groups = 1 + bs
    assert n_groups <= 8

    x = jnp.concatenate([x_prompt.reshape(n_ctx, d), x_sample.reshape(bs * t_lat, d)], axis=0)
    cond8 = jnp.zeros((8, d), F32).at[0].set(c_ctx).at[1:n_groups].set(c)

    new_state = []
    for i, (mode, w_qkv, w_out, params) in enumerate(mixers):
        gain1, gain2 = norms[i]
        mod = adaln(cond8, *adas[i])[:n_groups].reshape(n_groups, 6, 1, d)
        sh1, sc1, g1, sh2, sc2, g2 = (mod[:, k] for k in range(6))
        n_kv = caches[i][0].shape[1]
        lam_init = 0.8 - 0.6 * math.exp(-0.3 * i)

        qkv = norm_mod_matmul(x, gain1, sh1, sc1, (w_qkv,), F32, n_ctx, t_lat)
        ctx_mode = "mha" if mode == "na" else mode
        ctx_params = () if mode == "na" else params
        o, new_k, new_v = ctx_attention(qkv, ctx_params, ctx_mode, lam_init, n_batch, seq, n_heads, n_kv, m)
        if mode == "na":
            o = na_latent_attention(o, qkv, *caches[i], params[0], n_ctx, bs, t_lat, n_heads)
        else:
            o = latent_attention(o, qkv, *caches[i], params, mode, lam_init, n_ctx, bs, t_lat, n_heads, n_kv)
        new_state += [new_k, new_v]
        x = matmul_residual(o, w_out, x, g1, n_ctx, t_lat)

        if i % 2 == 0:
            w_gate, w_up, w_down = ffns[i]
            u = norm_mod_matmul(x, gain2, sh2, sc2, (w_gate, w_up), BF16, n_ctx, t_lat)
            x = matmul_residual(u, w_down, x, g2, n_ctx, t_lat)
        else:
            x = moe_layer(x, gain2, sh2, sc2, g2, *ffns[i], n_ctx, t_lat)

    y = final_rmsnorm(x, final_gain)
    return (y[:n_ctx].reshape(n_batch, seq, d), y[n_ctx:].reshape(bs, t_lat, d), *new_state)
```

```python
import functools
import math

import numpy as np
import jax
import jax.numpy as jnp
from jax import lax
from jax.experimental import pallas as pl
from jax.experimental.pallas import tpu as pltpu

F32 = jnp.float32
BF16 = jnp.bfloat16

HEAD_DIM = 128
GRID_W = 64
NA_ROWS = 8
NA_COLS = 16
NA_QROWS = 4
NA_BAND = 12
TOP_K = 2
ROPE_THETA = 10000.0
NORM_EPS = 1e-6
MASK_VALUE = -1e30
LOG2E = math.log2(math.e)
VMEM_LIMIT_BYTES = 56 * 2**20
MOE_TILE = 256
CAST_ROWS = 256
KEY_CHUNK = 512


def _cparams(*sem):
    return pltpu.CompilerParams(dimension_semantics=sem, vmem_limit_bytes=VMEM_LIMIT_BYTES)


def _pick(n, pref, mult=128):
    t = min(pref, n)
    t -= t % mult
    while t > mult and n % t:
        t -= mult
    assert t >= mult and n % t == 0, (n, pref)
    return t


def _dot(a, b):
    return jnp.dot(a, b, preferred_element_type=F32)


def _dot_nt(a, b):
    return lax.dot_general(a, b, (((1,), (1,)), ((), ())), preferred_element_type=F32)


def _rms(x):
    return x * lax.rsqrt(jnp.mean(x * x, axis=-1, keepdims=True) + NORM_EPS)


def _modulate(x, gain, shift, scale):
    return _rms(x) * gain * (1.0 + scale) + shift


def _group_of_tile(i, tm, n_ctx, t_lat):
    row = i * tm
    return jnp.where(row < n_ctx, 0, 1 + (row - n_ctx) // t_lat)


def _stage_bf16(src_ref, dst_ref):
    rows = math.gcd(src_ref.shape[0], CAST_ROWS)

    def body(i, c):
        r = pl.multiple_of(i * rows, rows)
        dst_ref[pl.ds(r, rows), :] = src_ref[pl.ds(r, rows), :].astype(BF16)
        return c

    lax.fori_loop(0, src_ref.shape[0] // rows, body, 0)


def _adaln_kernel(c_ref, w_ref, b_ref, o_ref):
    c = c_ref[...]
    s = c * jax.nn.sigmoid(c)
    o_ref[...] = _dot(s.astype(BF16), w_ref[...].astype(BF16)) + b_ref[...]


def adaln(cond8, w, b):
    d, n = w.shape
    tn = _pick(n, 1024)
    return pl.pallas_call(
        _adaln_kernel,
        grid=(n // tn,),
        in_specs=[pl.BlockSpec((8, d), lambda j: (0, 0)),
                  pl.BlockSpec((d, tn), lambda j: (0, j)),
                  pl.BlockSpec((1, tn), lambda j: (0, j))],
        out_specs=pl.BlockSpec((8, tn), lambda j: (0, j)),
        out_shape=jax.ShapeDtypeStruct((8, n), F32),
        compiler_params=_cparams("parallel"),
        name="adaln",
    )(cond8, w, b.reshape(1, n))


def _modulate_kernel(x_ref, g_ref, sh_ref, sc_ref, o_ref):
    o_ref[...] = _modulate(x_ref[...], g_ref[...], sh_ref[0], sc_ref[0]).astype(o_ref.dtype)


def modulate_rows(x, gain, shift, scale, n_ctx, t_lat, tm=512):
    m, d = x.shape
    tm = _pick(math.gcd(n_ctx, t_lat), tm)
    grp = lambda i: (_group_of_tile(i, tm, n_ctx, t_lat), 0, 0)
    return pl.pallas_call(
        _modulate_kernel,
        grid=(m // tm,),
        in_specs=[pl.BlockSpec((tm, d), lambda i: (i, 0)),
                  pl.BlockSpec((1, d), lambda i: (0, 0)),
                  pl.BlockSpec((1, 1, d), grp),
                  pl.BlockSpec((1, 1, d), grp)],
        out_specs=pl.BlockSpec((tm, d), lambda i: (i, 0)),
        out_shape=jax.ShapeDtypeStruct((m, d), BF16),
        compiler_params=_cparams("parallel"),
        name="modulate",
    )(x, gain.reshape(1, d), shift, scale)


def _rw_matmul_kernel(*refs, n_w, epilogue):
    a_ref = refs[0]
    w_refs = refs[1:1 + n_w]
    extra = refs[1 + n_w:-1 - n_w]
    o_ref = refs[-1 - n_w]
    wb_refs = refs[-n_w:]

    @pl.when(pl.program_id(1) == 0)
    def _():
        for w_ref, wb_ref in zip(w_refs, wb_refs):
            _stage_bf16(w_ref, wb_ref)

    a = a_ref[...]
    if epilogue == "swiglu":
        g = _dot(a, wb_refs[0][...])
        u = _dot(a, wb_refs[1][...])
        o = g * jax.nn.sigmoid(g) * u
    elif epilogue == "residual":
        res_ref, gate_ref = extra
        o = res_ref[...] + gate_ref[0] * _dot(a, wb_refs[0][...])
    else:
        o = _dot(a, wb_refs[0][...])
    o_ref[...] = o.astype(o_ref.dtype)


def rw_matmul(a, ws, out_dtype, tm, tn, epilogue="none", res=None, gate=None, n_ctx=None, t_lat=None):
    m, kdim = a.shape
    n = ws[0].shape[1]
    tm = _pick(math.gcd(n_ctx, t_lat) if n_ctx else m, tm)
    tn = _pick(n, tn)
    in_specs = [pl.BlockSpec((tm, kdim), lambda j, i: (i, 0))]
    in_specs += [pl.BlockSpec((kdim, tn), lambda j, i: (0, j)) for _ in ws]
    args = [a, *ws]
    if epilogue == "residual":
        in_specs += [pl.BlockSpec((tm, tn), lambda j, i: (i, j)),
                     pl.BlockSpec((1, 1, tn), lambda j, i: (_group_of_tile(i, tm, n_ctx, t_lat), 0, j))]
        args += [res, gate]
    return pl.pallas_call(
        functools.partial(_rw_matmul_kernel, n_w=len(ws), epilogue=epilogue),
        grid=(n // tn, m // tm),
        in_specs=in_specs,
        out_specs=pl.BlockSpec((tm, tn), lambda j, i: (i, j)),
        out_shape=jax.ShapeDtypeStruct((m, n), out_dtype),
        scratch_shapes=[pltpu.VMEM((kdim, tn), BF16) for _ in ws],
        compiler_params=_cparams("parallel", "arbitrary"),
        name="matmul_" + epilogue,
    )(*args)


def _softmax_rows(s):
    p = jnp.exp(s - jnp.max(s, axis=-1, keepdims=True))
    return p * (1.0 / jnp.sum(p, axis=-1, keepdims=True))


def _diff_lambda(lq1_ref, lk1_ref, lq2_ref, lk2_ref, lam_init):
    a = jnp.sum(lq1_ref[...] * lk1_ref[...], axis=-1, keepdims=True)
    b = jnp.sum(lq2_ref[...] * lk2_ref[...], axis=-1, keepdims=True)
    return jnp.exp(a) - jnp.exp(b) + lam_init


def _split_maps(q, scale):
    first = lax.broadcasted_iota(jnp.int32, (1, HEAD_DIM), 1) < HEAD_DIM // 2
    q = q * scale
    return jnp.where(first, q, 0.0).astype(BF16), jnp.where(first, 0.0, q).astype(BF16)


def _diff_attend(q, k_bf, v_bf, lam, subln, lam_init):
    q1, q2 = _split_maps(q, 1.0)
    scale = (HEAD_DIM // 2) ** -0.5
    p1 = _softmax_rows(_dot_nt(q1, k_bf) * scale)
    p2 = _softmax_rows(_dot_nt(q2, k_bf) * scale)
    o = _dot((p1 - lam * p2).astype(BF16), v_bf)
    return _rms(o) * subln * (1.0 - lam_init)


def _plain_attend(q_bf, k_bf, v_bf):
    s = _dot_nt(q_bf, k_bf) * (HEAD_DIM ** -0.5)
    p = jnp.exp(s - jnp.max(s, axis=-1, keepdims=True))
    return _dot(p.astype(BF16), v_bf) * (1.0 / jnp.sum(p, axis=-1, keepdims=True))


def _online_softmax_pv(q_bf, k_ref, vaug_ref, chunks):
    nq = q_bf.shape[0]
    m = jnp.full((nq, 1), -jnp.inf, F32)
    acc = jnp.zeros((nq, 2 * HEAD_DIM), F32)
    for start, size in chunks:
        s = _dot_nt(q_bf, k_ref[start:start + size, :])
        m_new = jnp.maximum(m, jnp.max(s, axis=-1, keepdims=True))
        p = jnp.exp2(s - m_new)
        acc = acc * jnp.exp2(m - m_new) +_dot(p.astype(BF16), vaug_ref[start:start + size, :])
        m = m_new
    return acc


def _rope(x, cos, sin, quarter):
    lane = lax.broadcasted_iota(jnp.int32, (1, HEAD_DIM), 1)
    even = (lane // quarter) % 2 == 0
    nxt = pltpu.roll(x, HEAD_DIM - quarter, 1)
    prv = pltpu.roll(x, quarter, 1)
    return x * cos + jnp.where(even, -nxt, prv) * sin


def _rope_tables(t_lat, dim):
    quarter = dim // 4
    inv_freq = ROPE_THETA ** (-jnp.arange(quarter, dtype=F32) / quarter)
    t = jnp.arange(t_lat)
    ang_r = (t // GRID_W).astype(F32)[:, None] * inv_freq
    ang_c = (t % GRID_W).astype(F32)[:, None] * inv_freq
    ang = jnp.concatenate([ang_r, ang_r, ang_c, ang_c] * (HEAD_DIM // dim), axis=-1)
    return jnp.cos(ang), jnp.sin(ang)


def _ctx_attn_kernel(*refs, mode, lam_init):
    q_ref, k_ref, v_ref = refs[:3]
    o_ref, nk_ref, nv_ref = refs[-3:]
    q, k, v = q_ref[...], k_ref[...], v_ref[...]
    if mode == "gqa":
        qn_ref, kn_ref = refs[3:5]
        q = _rms(q) * qn_ref[...]
        k = _rms(k) * kn_ref[...]
    nk_ref[...] = k
    nv_ref[...] = v
    if mode == "diff":
        lam = _diff_lambda(*refs[3:7], lam_init)
        o = _diff_attend(q, k.astype(BF16), v.astype(BF16), lam, refs[7][...], lam_init)
    else:
        o = _plain_attend(q.astype(BF16), k.astype(BF16), v.astype(BF16))
    o_ref[...] = o.astype(o_ref.dtype)


def ctx_attention(qkv, params, mode, lam_init, n_batch, seq, n_heads, n_kv, m_total):
    grp = n_heads // n_kv
    blk = lambda f: pl.BlockSpec((seq, HEAD_DIM), f)
    small = [pl.BlockSpec((1, p.shape[-1]), lambda b, h: (0, 0)) for p in params]
    state = pl.BlockSpec((None, None, seq, HEAD_DIM), lambda b, h: (b, h // grp, 0, 0))
    st_shape = jax.ShapeDtypeStruct((n_batch, n_kv, seq, HEAD_DIM), F32)
    return pl.pallas_call(
        functools.partial(_ctx_attn_kernel, mode=mode, lam_init=lam_init),
        grid=(n_batch, n_heads),
        in_specs=[blk(lambda b, h: (b, h)),
                  blk(lambda b, h: (b, n_heads + h // grp)),
                  blk(lambda b, h: (b, n_heads + n_kv + h // grp))] + small,
        out_specs=[blk(lambda b, h: (b, h)), state, state],
        out_shape=[jax.ShapeDtypeStruct((m_total, n_heads * HEAD_DIM), BF16), st_shape, st_shape],
        compiler_params=_cparams("parallel", "arbitrary"),
        name="ctx_attention_" + mode,
    )(qkv, qkv, qkv, *[p.reshape(1, -1) for p in params])


def _lat_attn_kernel(*refs, mode, lam_init, n_ctx_keys, grp, quarter, chunks):
    (o_in_ref, q_ref, kl_ref, vl_ref, ck_ref, cv_ref,
     cosq_ref, sinq_ref, cosk_ref, sink_ref) = refs[:10]
    o_ref, kall_ref, vaug_ref = refs[-3:]
    del o_in_ref
    params = refs[10:-3]
    h, qi = pl.program_id(1), pl.program_id(2)

    @pl.when(jnp.logical_and(qi == 0, h % grp == 0))
    def _():
        k = kl_ref[...]
        if mode == "gqa":
            k = _rms(k) * params[1][...]
        kall_ref[:n_ctx_keys, :] = ck_ref[...].astype(BF16)
        kall_ref[n_ctx_keys:, :] = _rope(k, cosk_ref[...], sink_ref[...], quarter).astype(BF16)
        vaug_ref[:n_ctx_keys, :HEAD_DIM] = cv_ref[...].astype(BF16)
        vaug_ref[n_ctx_keys:, :HEAD_DIM] = vl_ref[...].astype(BF16)
        vaug_ref[:, HEAD_DIM:] = jnp.ones((vaug_ref.shape[0], HEAD_DIM), BF16)

    q = q_ref[...]
    if mode == "gqa":
        q = _rms(q) * params[0][...]
    q = _rope(q, cosq_ref[...], sinq_ref[...], quarter)
    if mode == "diff":
        lam = _diff_lambda(*params[:4], lam_init)
        q1, q2 = _split_maps(q, (HEAD_DIM // 2) ** -0.5 * LOG2E)
        a1 = _online_softmax_pv(q1, kall_ref, vaug_ref, chunks)
        a2 = _online_softmax_pv(q2, kall_ref, vaug_ref, chunks)
        o = a1[:, :HEAD_DIM] * (1.0 / a1[:, HEAD_DIM:]) - a2[:, :HEAD_DIM] * (lam / a2[:, HEAD_DIM:])
        o = _rms(o) * params[4][...] * (1.0 - lam_init)
    else:
        a = _online_softmax_pv((q * (HEAD_DIM ** -0.5 * LOG2E)).astype(BF16), kall_ref, vaug_ref, chunks)
        o = a[:, :HEAD_DIM] * (1.0 / a[:, HEAD_DIM:])
    o_ref[...] = o.astype(o_ref.dtype)


def latent_attention(o_ctx, qkv, cache_k, cache_v, params, mode, lam_init,
                     n_ctx, n_batch, t_lat, n_heads, n_kv, tq=256):
    grp = n_heads // n_kv
    n_past = cache_k.shape[2]
    tq = _pick(math.gcd(n_ctx, t_lat), tq)
    nq = t_lat // tq
    assert n_ctx % t_lat == 0
    base_q, base_k = n_ctx // tq, n_ctx // t_lat
    dim = HEAD_DIM // 2 if mode == "diff" else HEAD_DIM
    cos, sin = _rope_tables(t_lat, dim)
    kc = _pick(t_lat, KEY_CHUNK)
    chunks = ((0, n_past),) + tuple((n_past + i * kc, kc) for i in range(t_lat // kc))
    qblk = pl.BlockSpec((tq, HEAD_DIM), lambda b, h, i: (base_q + b * nq + i, h))
    kvblk = lambda off: pl.BlockSpec((t_lat, HEAD_DIM), lambda b, h, i: (base_k + b, off + h // grp))
    cblk = pl.BlockSpec((None, None, n_past, HEAD_DIM), lambda b, h, i: (b, h // grp, 0, 0))
    tabq = pl.BlockSpec((tq, HEAD_DIM), lambda b, h, i: (i, 0))
    tabk = pl.BlockSpec((t_lat, HEAD_DIM), lambda b, h, i: (0, 0))
    small = [pl.BlockSpec((1, p.shape[-1]), lambda b, h, i: (0, 0)) for p in params]
    return pl.pallas_call(
        functools.partial(_lat_attn_kernel, mode=mode, lam_init=lam_init,
                          n_ctx_keys=n_past, grp=grp, quarter=dim // 4, chunks=chunks),
        grid=(n_batch, n_heads, nq),
        in_specs=[pl.BlockSpec(memory_space=pl.ANY), qblk, kvblk(n_heads), kvblk(n_heads + n_kv),
                  cblk, cblk, tabq, tabq, tabk, tabk] + small,
        out_specs=qblk,
        out_shape=jax.ShapeDtypeStruct(o_ctx.shape, o_ctx.dtype),
        scratch_shapes=[pltpu.VMEM((n_past + t_lat, HEAD_DIM), BF16),
                        pltpu.VMEM((n_past + t_lat, 2 * HEAD_DIM), BF16)],
        input_output_aliases={0: 0},
        compiler_params=_cparams("parallel", "arbitrary", "arbitrary"),
        name="latent_attention_" + mode,
    )(o_ctx, qkv, qkv, qkv, cache_k, cache_v, cos, sin, cos, sin,
      *[p.reshape(1, -1) for p in params])


def _na_geometry(rows):
    kh = min(NA_ROWS, rows)
    band = min(NA_BAND, rows)
    nblk = rows // NA_QROWS
    qr = np.arange(NA_QROWS)[:, None]
    kr = np.arange(band)[None, :]
    qc = np.arange(GRID_W)[:, None]
    kc = np.arange(GRID_W)[None, :]
    cs = np.clip(qc - NA_COLS // 2, 0, GRID_W - NA_COLS)
    col_ok = (kc >= cs) & (kc < cs + NA_COLS)
    col_idx = np.where(col_ok, kc - qc + NA_COLS - 1, 0)
    starts, pat_ids, pats, keys = [], [], [], []
    for j in range(nblk):
        r0 = j * NA_QROWS
        bs = int(np.clip(r0 - kh // 2, 0, rows - band))
        r = r0 + qr
        rs = np.clip(r - kh // 2, 0, rows - kh)
        key_row = bs + kr
        row_ok = (key_row >= rs) & (key_row < rs + kh)
        row_idx = np.where(row_ok, key_row - r + NA_ROWS - 1, 0)
        key = (bs - r0, tuple((rs - r0).ravel()))
        if key not in keys:
            keys.append(key)
            pats.append((row_ok[:, None, :, None] & col_ok[None, :, None, :], row_idx))
        starts.append(bs)
        pat_ids.append(keys.index(key))
    return starts, pat_ids, pats, col_idx


def _na_bias_tables(rpb, rows):
    starts, pat_ids, pats, col_idx = _na_geometry(rows)
    n_heads = rpb.shape[0]
    col_onehot = jnp.asarray(col_idx[:, :, None] == np.arange(rpb.shape[2]), F32)
    tables = []
    for valid, row_idx in pats:
        by_row = rpb[:, row_idx, :]
        full = jnp.einsum("hqkj,cdj->hqckd", by_row, col_onehot, precision=lax.Precision.HIGHEST)
        full = jnp.where(valid[None], full, MASK_VALUE)
        tables.append(full.reshape(n_heads, valid.shape[0] * GRID_W, valid.shape[2] * GRID_W))
    return starts, pat_ids, jnp.stack(tables)


def _na_attn_kernel(start_ref, pat_ref, o_in_ref, q_ref, kl_ref, vl_ref, ck_ref, cv_ref, bias_ref,
                    o_ref, *, band_keys):
    del pat_ref, o_in_ref
    start = pl.multiple_of(start_ref[pl.program_id(2)] * GRID_W, GRID_W)
    scale = HEAD_DIM ** -0.5
    q = q_ref[...].astype(BF16)
    kb = kl_ref[pl.ds(start, band_keys), :].astype(BF16)
    vb = vl_ref[pl.ds(start, band_keys), :].astype(BF16)
    s_nb = _dot_nt(q, kb) * scale + bias_ref[...]
    s_cx = _dot_nt(q, ck_ref[...].astype(BF16)) * scale
    m = jnp.maximum(jnp.max(s_nb, axis=-1, keepdims=True), jnp.max(s_cx, axis=-1, keepdims=True))
    p_nb = jnp.exp(s_nb - m)
    p_cx = jnp.exp(s_cx - m)
    l = jnp.sum(p_nb, axis=-1, keepdims=True) + jnp.sum(p_cx, axis=-1, keepdims=True)
    o = _dot(p_nb.astype(BF16), vb) + _dot(p_cx.astype(BF16), cv_ref[...].astype(BF16))
    o_ref[...] = (o * (1.0 / l)).astype(o_ref.dtype)


def na_latent_attention(o_ctx, qkv, cache_k, cache_v, rpb, n_ctx, n_batch, t_lat, n_heads):
    rows = t_lat // GRID_W
    assert rows % NA_QROWS == 0 and n_ctx % t_lat == 0
    starts, pat_ids, bias = _na_bias_tables(rpb, rows)
    nblk = len(starts)
    tq = NA_QROWS * GRID_W
    band_keys = min(NA_BAND, rows) * GRID_W
    n_past = cache_k.shape[2]
    base_q, base_k = n_ctx // tq, n_ctx // t_lat
    qblk = pl.BlockSpec((tq, HEAD_DIM), lambda b, h, j, st, pt: (base_q + b * nblk + j, h))
    kvblk = lambda off: pl.BlockSpec((t_lat, HEAD_DIM), lambda b, h, j, st, pt: (base_k + b, off + h))
    cblk = pl.BlockSpec((None, None, n_past, HEAD_DIM), lambda b, h, j, st, pt: (b, h, 0, 0))
    bblk = pl.BlockSpec((None, None, tq, band_keys), lambda b, h, j, st, pt: (pt[j], h, 0, 0))
    return pl.pallas_call(
        functools.partial(_na_attn_kernel, band_keys=band_keys),
        grid_spec=pltpu.PrefetchScalarGridSpec(
            num_scalar_prefetch=2,
            grid=(n_batch, n_heads, nblk),
            in_specs=[pl.BlockSpec(memory_space=pl.ANY), qblk, kvblk(n_heads), kvblk(2 * n_heads),
                      cblk, cblk, bblk],
            out_specs=qblk),
        out_shape=jax.ShapeDtypeStruct(o_ctx.shape, o_ctx.dtype),
        input_output_aliases={2: 0},
        compiler_params=_cparams("parallel", "arbitrary", "arbitrary"),
        name="latent_attention_na",
    )(jnp.asarray(starts, jnp.int32), jnp.asarray(pat_ids, jnp.int32),
      o_ctx, qkv, qkv, qkv, cache_k, cache_v, bias)


def _router_kernel(x_ref, g_ref, sh_ref, sc_ref, rt_ref, h_ref, idx_ref, gate_ref):
    h = _modulate(x_ref[...], g_ref[...], sh_ref[0], sc_ref[0])
    h_ref[...] = h
    logits = lax.dot_general(rt_ref[...], h, (((1,), (1,)), ((), ())),
                             precision=lax.Precision.HIGHEST, preferred_element_type=F32)
    n_exp = logits.shape[0]
    eid = lax.broadcasted_iota(jnp.int32, logits.shape, 0)
    m1 = jnp.max(logits, axis=0, keepdims=True)
    i1 = jnp.min(jnp.where(logits == m1, eid, n_exp), axis=0, keepdims=True)
    rest = jnp.where(eid == i1, -jnp.inf, logits)
    m2 = jnp.max(rest, axis=0, keepdims=True)
    i2 = jnp.min(jnp.where(rest == m2, eid, n_exp), axis=0, keepdims=True)
    e = jnp.exp(m2 - m1)
    idx_ref[0:1, :] = i1
    idx_ref[1:2, :] = i2
    gate_ref[0:1, :] = 1.0 / (1.0 + e)
    gate_ref[1:2, :] = e / (1.0 + e)


def moe_route(x, gain, shift, scale, router, n_ctx, t_lat, tm=512):
    m, d = x.shape
    n_exp = router.shape[1]
    tm = _pick(math.gcd(n_ctx, t_lat), tm)
    grp = lambda i: (_group_of_tile(i, tm, n_ctx, t_lat), 0, 0)
    return pl.pallas_call(
        _router_kernel,
        grid=(m // tm,),
        in_specs=[pl.BlockSpec((tm, d), lambda i: (i, 0)),
                  pl.BlockSpec((1, d), lambda i: (0, 0)),
                  pl.BlockSpec((1, 1, d), grp),
                  pl.BlockSpec((1, 1, d), grp),
                  pl.BlockSpec((n_exp, d), lambda i: (0, 0))],
        out_specs=[pl.BlockSpec((tm, d), lambda i: (i, 0)),
                   pl.BlockSpec((TOP_K, tm), lambda i: (0, i)),
                   pl.BlockSpec((TOP_K, tm), lambda i: (0, i))],
        out_shape=[jax.ShapeDtypeStruct((m, d), F32),
                   jax.ShapeDtypeStruct((TOP_K, m), jnp.int32),
                   jax.ShapeDtypeStruct((TOP_K, m), F32)],
        compiler_params=_cparams("parallel"),
        name="moe_router",
    )(x, gain.reshape(1, d), shift, scale, router.T)


def _gather_rows_kernel(src_ref, h_hbm, o_ref, buf_ref, sem):
    t = pl.program_id(0)
    rows = buf_ref.shape[0]

    def row_copy(r):
        return pltpu.make_async_copy(h_hbm.at[pl.ds(src_ref[t * rows + r], 1)],
                                     buf_ref.at[pl.ds(r, 1)], sem)

    def start(r, c):
        row_copy(r).start()
        return c

    def wait(r, c):
        row_copy(r).wait()
        return c

    lax.fori_loop(0, rows, start, 0)
    lax.fori_loop(0, rows, wait, 0)
    o_ref[...] = buf_ref[...].astype(o_ref.dtype)


def gather_rows(h, src, n_tiles):
    d = h.shape[1]
    return pl.pallas_call(
        _gather_rows_kernel,
        grid_spec=pltpu.PrefetchScalarGridSpec(
            num_scalar_prefetch=1,
            grid=(n_tiles,),
            in_specs=[pl.BlockSpec(memory_space=pl.ANY)],
            out_specs=pl.BlockSpec((MOE_TILE, d), lambda t, s: (t, 0)),
            scratch_shapes=[pltpu.VMEM((MOE_TILE, d), h.dtype), pltpu.SemaphoreType.DMA(())]),
        out_shape=jax.ShapeDtypeStruct((n_tiles * MOE_TILE, d), BF16),
        compiler_params=_cparams("arbitrary"),
        name="moe_gather",
    )(src, h)


def _expert_changed(te_ref, nu_ref):
    t = pl.program_id(1)
    first = jnp.logical_or(t == 0, te_ref[t] != te_ref[jnp.maximum(t - 1, 0)])
    return jnp.logical_and(first, t < nu_ref[0])


def _expert_up_kernel(te_ref, nu_ref, x_ref, wg_ref, wu_ref, o_ref, wgb_ref, wub_ref):
    @pl.when(_expert_changed(te_ref, nu_ref))
    def _():
        _stage_bf16(wg_ref, wgb_ref)
        _stage_bf16(wu_ref, wub_ref)

    @pl.when(pl.program_id(1) < nu_ref[0])
    def _():
        x = x_ref[...]
        a = _dot(x, wgb_ref[...])
        b = _dot(x, wub_ref[...])
        o_ref[...] = (a * jax.nn.sigmoid(a) * b).astype(o_ref.dtype)

    @pl.when(pl.program_id(1) >= nu_ref[0])
    def _():
        o_ref[...] = jnp.zeros_like(o_ref)


def expert_up(xs, w_gate, w_up, tile_expert, n_used, tf=1024):
    p, d = xs.shape
    f = w_gate.shape[2]
    tf = _pick(f, tf)
    nt = p // MOE_TILE
    used = lambda t, nu: jnp.minimum(t, nu[0] - 1)
    wblk = pl.BlockSpec((None, d, tf), lambda j, t, te, nu: (te[used(t, nu)], 0, j))
    return pl.pallas_call(
        _expert_up_kernel,
        grid_spec=pltpu.PrefetchScalarGridSpec(
            num_scalar_prefetch=2,
            grid=(f // tf, nt),
            in_specs=[pl.BlockSpec((MOE_TILE, d), lambda j, t, te, nu: (used(t, nu), 0)), wblk, wblk],
            out_specs=pl.BlockSpec((MOE_TILE, tf), lambda j, t, te, nu: (t, j)),
            scratch_shapes=[pltpu.VMEM((d, tf), BF16), pltpu.VMEM((d, tf), BF16)]),
        out_shape=jax.ShapeDtypeStruct((p, f), BF16),
        compiler_params=_cparams("parallel", "arbitrary"),
        name="moe_expert_up",
    )(tile_expert, n_used, xs, w_gate, w_up)


def _expert_down_kernel(te_ref, nu_ref, u_ref, w_ref, g_ref, o_ref, wb_ref):
    @pl.when(_expert_changed(te_ref, nu_ref))
    def _():
        _stage_bf16(w_ref, wb_ref)

    @pl.when(pl.program_id(1) < nu_ref[0])
    def _():
        o_ref[...] = _dot(u_ref[...], wb_ref[...]) * g_ref[...]

    @pl.when(pl.program_id(1) >= nu_ref[0])
    def _():
        o_ref[...] = jnp.zeros_like(o_ref)


def expert_down(u, w_down, gate_rows, tile_expert, n_used, tn=512):
    p, f = u.shape
    d = w_down.shape[2]
    tn = _pick(d, tn)
    nt = p // MOE_TILE
    used = lambda t, nu: jnp.minimum(t, nu[0] - 1)
    return pl.pallas_call(
        _expert_down_kernel,
        grid_spec=pltpu.PrefetchScalarGridSpec(
            num_scalar_prefetch=2,
            grid=(d // tn, nt),
            in_specs=[pl.BlockSpec((MOE_TILE, f), lambda j, t, te, nu: (used(t, nu), 0)),
                      pl.BlockSpec((None, f, tn), lambda j, t, te, nu: (te[used(t, nu)], 0, j)),
                      pl.BlockSpec((MOE_TILE, 1), lambda j, t, te, nu: (used(t, nu), 0))],
            out_specs=pl.BlockSpec((MOE_TILE, tn), lambda j, t, te, nu: (t, j)),
            scratch_shapes=[pltpu.VMEM((f, tn), BF16)]),
        out_shape=jax.ShapeDtypeStruct((p, d), F32),
        compiler_params=_cparams("parallel", "arbitrary"),
        name="moe_expert_down",
    )(tile_expert, n_used, u, w_down, gate_rows)


def _combine_kernel(p0_ref, p1_ref, x_ref, g_ref, y_hbm, o_ref, ya_ref, yb_ref, sem):
    i = pl.program_id(0)
    rows = ya_ref.shape[0]

    def copies(r):
        return (pltpu.make_async_copy(y_hbm.at[pl.ds(p0_ref[i * rows + r], 1)],
                                      ya_ref.at[pl.ds(r, 1)], sem.at[0]),
                pltpu.make_async_copy(y_hbm.at[pl.ds(p1_ref[i * rows + r], 1)],
                                      yb_ref.at[pl.ds(r, 1)], sem.at[1]))

    def start(r, c):
        for cp in copies(r):
            cp.start()
        return c

    def wait(r, c):
        for cp in copies(r):
            cp.wait()
        return c

    lax.fori_loop(0, rows, start, 0)
    lax.fori_loop(0, rows, wait, 0)
    o_ref[...] = x_ref[...] + g_ref[0] * (ya_ref[...] + yb_ref[...])


def moe_combine(x, gate, y, pos0, pos1, n_ctx, t_lat, tm=256):
    m, d = x.shape
    tm = _pick(math.gcd(n_ctx, t_lat), tm)
    return pl.pallas_call(
        _combine_kernel,
        grid_spec=pltpu.PrefetchScalarGridSpec(
            num_scalar_prefetch=2,
            grid=(m // tm,),
            in_specs=[pl.BlockSpec((tm, d), lambda i, a, b: (i, 0)),
                      pl.BlockSpec((1, 1, d), lambda i, a, b: (_group_of_tile(i, tm, n_ctx, t_lat), 0, 0)),
                      pl.BlockSpec(memory_space=pl.ANY)],
            out_specs=pl.BlockSpec((tm, d), lambda i, a, b: (i, 0)),
            scratch_shapes=[pltpu.VMEM((tm, d), F32), pltpu.VMEM((tm, d), F32),
                            pltpu.SemaphoreType.DMA((2,))]),
        out_shape=jax.ShapeDtypeStruct((m, d), F32),
        compiler_params=_cparams("arbitrary"),
        name="moe_combine",
    )(pos0, pos1, x, gate, y)


def moe_layer(x, gain, shift, scale, gate, router, w_gate, w_up, w_down, n_ctx, t_lat):
    m, _ = x.shape
    n_exp = router.shape[1]
    h, idx, gates = moe_route(x, gain, shift, scale, router, n_ctx, t_lat)
    expert = idx.T.reshape(-1)
    n_assign = expert.shape[0]
    onehot = (expert[:, None] == jnp.arange(n_exp)[None, :]).astype(jnp.int32)
    csum = jnp.cumsum(onehot, axis=0)
    rank = jnp.sum(onehot * (csum - onehot), axis=1)
    counts = csum[-1]
    padded = (counts + MOE_TILE - 1) // MOE_TILE * MOE_TILE
    pad_end = jnp.cumsum(padded)
    dest = jnp.sum(onehot * (pad_end - padded)[None, :], axis=1) + rank
    n_tiles = n_assign // MOE_TILE + n_exp
    token = jnp.arange(n_assign, dtype=jnp.int32) // TOP_K
    src = jnp.zeros((n_tiles * MOE_TILE,), jnp.int32).at[dest].set(token)
    gate_rows = jnp.zeros((n_tiles * MOE_TILE, 1), F32).at[dest, 0].set(gates.T.reshape(-1))
    tile_start = jnp.arange(n_tiles, dtype=jnp.int32) * MOE_TILE
    tile_expert = jnp.minimum(jnp.sum((pad_end[None, :] <= tile_start[:, None]).astype(jnp.int32), axis=1),
                              n_exp - 1)
    n_used = (pad_end[-1:] // MOE_TILE).astype(jnp.int32)
    pos = dest.reshape(m, TOP_K).astype(jnp.int32)

    xs = gather_rows(h, src, n_tiles)
    u = expert_up(xs, w_gate, w_up, tile_expert, n_used)
    y = expert_down(u, w_down, gate_rows, tile_expert, n_used)
    return moe_combine(x, gate, y, pos[:, 0], pos[:, 1], n_ctx, t_lat)


def _final_norm_kernel(x_ref, g_ref, o_ref):
    o_ref[...] = _rms(x_ref[...]) * g_ref[...]


def final_rmsnorm(x, gain, tm=512):
    m, d = x.shape
    tm = _pick(m, tm)
    return pl.pallas_call(
        _final_norm_kernel,
        grid=(m // tm,),
        in_specs=[pl.BlockSpec((tm, d), lambda i: (i, 0)), pl.BlockSpec((1, d), lambda i: (0, 0))],
        out_specs=pl.BlockSpec((tm, d), lambda i: (i, 0)),
        out_shape=jax.ShapeDtypeStruct((m, d), F32),
        compiler_params=_cparams("parallel"),
        name="final_norm",
    )(x, gain.reshape(1, d))


def kernel(x_prompt, x_sample, c, cache_k_l0, cache_v_l0, cache_k_l1, cache_v_l1, cache_k_l2, cache_v_l2, cache_k_l3, cache_v_l3, c_ctx, final_norm, norm1_l0, norm2_l0, ada_w_l0, ada_b_l0, attn_qkv_l0, attn_out_l0, lambda_q1_l0, lambda_k1_l0, lambda_q2_l0, lambda_k2_l0, subln_l0, ffn_gate_l0, ffn_up_l0, ffn_down_l0, norm1_l1, norm2_l1, ada_w_l1, ada_b_l1, attn_qkv_l1, attn_out_l1, q_norm_l1, k_norm_l1, router_l1, moe_gate_l1, moe_up_l1, moe_down_l1, norm1_l2, norm2_l2, ada_w_l2, ada_b_l2, attn_qkv_l2, attn_out_l2, rpb_l2, ffn_gate_l2, ffn_up_l2, ffn_down_l2, norm1_l3, norm2_l3, ada_w_l3, ada_b_l3, attn_qkv_l3, attn_out_l3, lambda_q1_l3, lambda_k1_l3, lambda_q2_l3, lambda_k2_l3, subln_l3, router_l3, moe_gate_l3, moe_up_l3, moe_down_l3):
    final_gain = final_norm
    mixers = [
        ("diff", attn_qkv_l0, attn_out_l0, (lambda_q1_l0, lambda_k1_l0, lambda_q2_l0, lambda_k2_l0, subln_l0)),
        ("gqa", attn_qkv_l1, attn_out_l1, (q_norm_l1, k_norm_l1)),
        ("na", attn_qkv_l2, attn_out_l2, (rpb_l2,)),
        ("diff", attn_qkv_l3, attn_out_l3, (lambda_q1_l3, lambda_k1_l3, lambda_q2_l3, lambda_k2_l3, subln_l3)),
    ]
    ffns = [
        (ffn_gate_l0, ffn_up_l0, ffn_down_l0),
        (router_l1, moe_gate_l1, moe_up_l1, moe_down_l1),
        (ffn_gate_l2, ffn_up_l2, ffn_down_l2),
        (router_l3, moe_gate_l3, moe_up_l3, moe_down_l3),
    ]
    caches = [(cache_k_l0, cache_v_l0), (cache_k_l1, cache_v_l1),
              (cache_k_l2, cache_v_l2), (cache_k_l3, cache_v_l3)]
    norms = [(norm1_l0, norm2_l0), (norm1_l1, norm2_l1), (norm1_l2, norm2_l2), (norm1_l3, norm2_l3)]
    adas = [(ada_w_l0, ada_b_l0), (ada_w_l1, ada_b_l1), (ada_w_l2, ada_b_l2), (ada_w_l3, ada_b_l3)]

    n_batch, seq, d = x_prompt.shape
    bs, t_lat, _ = x_sample.shape
    n_ctx = n_batch * seq
    m = n_ctx + bs * t_lat
    n_heads = d // HEAD_DIM
    n_groups = 1 + bs
    assert n_groups <= 8
    rows_of = dict(n_ctx=n_ctx, t_lat=t_lat)

    x = jnp.concatenate([x_prompt.reshape(n_ctx, d), x_sample.reshape(bs * t_lat, d)], axis=0)
    cond8 = jnp.zeros((8, d), F32).at[0].set(c_ctx).at[1:n_groups].set(c)

    new_state = []
    for i, (mode, w_qkv, w_out, params) in enumerate(mixers):
        gain1, gain2 = norms[i]
        mod = adaln(cond8, *adas[i])[:n_groups].reshape(n_groups, 6, 1, d)
        sh1, sc1, g1, sh2, sc2, g2 = (mod[:, k] for k in range(6))
        n_kv = caches[i][0].shape[1]
        lam_init = 0.8 - 0.6 * math.exp(-0.3 * i)

        h = modulate_rows(x, gain1, sh1, sc1, n_ctx, t_lat)
        qkv = rw_matmul(h, (w_qkv,), F32, tm=1024, tn=1024, **rows_of)
        ctx_mode = "mha" if mode == "na" else mode
        ctx_params = () if mode == "na" else params
        o, new_k, new_v = ctx_attention(qkv, ctx_params, ctx_mode, lam_init, n_batch, seq, n_heads, n_kv, m)
        if mode == "na":
            o = na_latent_attention(o, qkv, *caches[i], params[0], n_ctx, bs, t_lat, n_heads)
        else:
            o = latent_attention(o, qkv, *caches[i], params, mode, lam_init, n_ctx, bs, t_lat, n_heads, n_kv)
        new_state += [new_k, new_v]
        x = rw_matmul(o, (w_out,), F32, tm=512, tn=1024, epilogue="residual", res=x, gate=g1, **rows_of)

        if i % 2 == 0:
            w_gate, w_up, w_down = ffns[i]
            h = modulate_rows(x, gain2, sh2, sc2, n_ctx, t_lat)
            u = rw_matmul(h, (w_gate, w_up), BF16, tm=1024, tn=512, epilogue="swiglu", **rows_of)
            x = rw_matmul(u, (w_down,), F32, tm=256, tn=512, epilogue="residual", res=x, gate=g2, **rows_of)
        else:
            x = moe_layer(x, gain2, sh2, sc2, g2, *ffns[i], n_ctx, t_lat)

    y = final_rmsnorm(x, final_gain)
    return (y[:n_ctx].reshape(n_batch, seq, d), y[n_ctx:].reshape(bs, t_lat, d), *new_state)
```

```python
import functools
import math

import numpy as np
import jax
import jax.numpy as jnp
from jax import lax
from jax.experimental import pallas as pl
from jax.experimental.pallas import tpu as pltpu

F32 = jnp.float32
BF16 = jnp.bfloat16

HEAD_DIM = 128
GRID_W = 64
NA_ROWS = 8
NA_COLS = 16
NA_QROWS = 4
NA_BAND = 12
TOP_K = 2
ROPE_THETA = 10000.0
NORM_EPS = 1e-6
MASK_VALUE = -1e30
LOG2E = math.log2(math.e)
VMEM_LIMIT_BYTES = 56 * 2**20
MOE_TILE = 256
CAST_ROWS = 256
KEY_CHUNK = 512
CTX_HEADS_PER_STEP = 4


def _cparams(*sem):
    return pltpu.CompilerParams(dimension_semantics=sem, vmem_limit_bytes=VMEM_LIMIT_BYTES)


def _pick(n, pref, mult=128):
    t = min(pref, n)
    t -= t % mult
    while t > mult and n % t:
        t -= mult
    assert t >= mult and n % t == 0, (n, pref)
    return t


def _dot(a, b):
    return jnp.dot(a, b, preferred_element_type=F32)


def _dot_nt(a, b):
    return lax.dot_general(a, b, (((1,), (1,)), ((), ())), preferred_element_type=F32)


def _rms(x):
    return x * lax.rsqrt(jnp.mean(x * x, axis=-1, keepdims=True) + NORM_EPS)


def _modulate(x, gain, shift, scale):
    return _rms(x) * gain * (1.0 + scale) + shift


def _group_of_tile(i, tm, n_ctx, t_lat):
    row = i * tm
    return jnp.where(row < n_ctx, 0, 1 + (row - n_ctx) // t_lat)


def _stage_bf16(src_ref, dst_ref):
    rows = math.gcd(src_ref.shape[0], CAST_ROWS)

    def body(i, c):
        r = pl.multiple_of(i * rows, rows)
        dst_ref[pl.ds(r, rows), :] = src_ref[pl.ds(r, rows), :].astype(BF16)
        return c

    lax.fori_loop(0, src_ref.shape[0] // rows, body, 0)


def _adaln_kernel(c_ref, w_ref, b_ref, o_ref):
    c = c_ref[...]
    s = c * jax.nn.sigmoid(c)
    o_ref[...] = _dot(s.astype(BF16), w_ref[...].astype(BF16)) + b_ref[...]


def adaln(cond8, w, b):
    d, n = w.shape
    tn = _pick(n, 1024)
    return pl.pallas_call(
        _adaln_kernel,
        grid=(n // tn,),
        in_specs=[pl.BlockSpec((8, d), lambda j: (0, 0)),
                  pl.BlockSpec((d, tn), lambda j: (0, j)),
                  pl.BlockSpec((1, tn), lambda j: (0, j))],
        out_specs=pl.BlockSpec((8, tn), lambda j: (0, j)),
        out_shape=jax.ShapeDtypeStruct((8, n), F32),
        compiler_params=_cparams("parallel"),
        name="adaln",
    )(cond8, w, b.reshape(1, n))


def _modulate_kernel(x_ref, g_ref, sh_ref, sc_ref, o_ref):
    o_ref[...] = _modulate(x_ref[...], g_ref[...], sh_ref[0], sc_ref[0]).astype(o_ref.dtype)


def modulate_rows(x, gain, shift, scale, n_ctx, t_lat, tm=512):
    m, d = x.shape
    tm = _pick(math.gcd(n_ctx, t_lat), tm)
    grp = lambda i: (_group_of_tile(i, tm, n_ctx, t_lat), 0, 0)
    return pl.pallas_call(
        _modulate_kernel,
        grid=(m // tm,),
        in_specs=[pl.BlockSpec((tm, d), lambda i: (i, 0)),
                  pl.BlockSpec((1, d), lambda i: (0, 0)),
                  pl.BlockSpec((1, 1, d), grp),
                  pl.BlockSpec((1, 1, d), grp)],
        out_specs=pl.BlockSpec((tm, d), lambda i: (i, 0)),
        out_shape=jax.ShapeDtypeStruct((m, d), BF16),
        compiler_params=_cparams("parallel"),
        name="modulate",
    )(x, gain.reshape(1, d), shift, scale)


def _rw_matmul_kernel(*refs, n_w, epilogue):
    a_ref = refs[0]
    w_refs = refs[1:1 + n_w]
    extra = refs[1 + n_w:-1 - n_w]
    o_ref = refs[-1 - n_w]
    wb_refs = refs[-n_w:]

    @pl.when(pl.program_id(1) == 0)
    def _():
        for w_ref, wb_ref in zip(w_refs, wb_refs):
            _stage_bf16(w_ref, wb_ref)

    a = a_ref[...]
    if epilogue == "swiglu":
        g = _dot(a, wb_refs[0][...])
        u = _dot(a, wb_refs[1][...])
        o = g * jax.nn.sigmoid(g) * u
    elif epilogue == "residual":
        res_ref, gate_ref = extra
        o = res_ref[...] + gate_ref[0] * _dot(a, wb_refs[0][...])
    else:
        o = _dot(a, wb_refs[0][...])
    o_ref[...] = o.astype(o_ref.dtype)


def rw_matmul(a, ws, out_dtype, tm, tn, epilogue="none", res=None, gate=None, n_ctx=None, t_lat=None):
    m, kdim = a.shape
    n = ws[0].shape[1]
    tm = _pick(math.gcd(n_ctx, t_lat) if n_ctx else m, tm)
    tn = _pick(n, tn)
    in_specs = [pl.BlockSpec((tm, kdim), lambda j, i: (i, 0))]
    in_specs += [pl.BlockSpec((kdim, tn), lambda j, i: (0, j)) for _ in ws]
    args = [a, *ws]
    if epilogue == "residual":
        in_specs += [pl.BlockSpec((tm, tn), lambda j, i: (i, j)),
                     pl.BlockSpec((1, 1, tn), lambda j, i: (_group_of_tile(i, tm, n_ctx, t_lat), 0, j))]
        args += [res, gate]
    return pl.pallas_call(
        functools.partial(_rw_matmul_kernel, n_w=len(ws), epilogue=epilogue),
        grid=(n // tn, m // tm),
        in_specs=in_specs,
        out_specs=pl.BlockSpec((tm, tn), lambda j, i: (i, j)),
        out_shape=jax.ShapeDtypeStruct((m, n), out_dtype),
        scratch_shapes=[pltpu.VMEM((kdim, tn), BF16) for _ in ws],
        compiler_params=_cparams("parallel", "arbitrary"),
        name="matmul_" + epilogue,
    )(*args)


def _softmax_rows(s):
    p = jnp.exp(s - jnp.max(s, axis=-1, keepdims=True))
    return p * (1.0 / jnp.sum(p, axis=-1, keepdims=True))


def _diff_lambda(lq1_ref, lk1_ref, lq2_ref, lk2_ref, lam_init):
    a = jnp.sum(lq1_ref[...] * lk1_ref[...], axis=-1, keepdims=True)
    b = jnp.sum(lq2_ref[...] * lk2_ref[...], axis=-1, keepdims=True)
    return jnp.exp(a) - jnp.exp(b) + lam_init


def _split_maps(q, scale):
    first = lax.broadcasted_iota(jnp.int32, (1, HEAD_DIM), 1) < HEAD_DIM // 2
    q = q * scale
    return jnp.where(first, q, 0.0).astype(BF16), jnp.where(first, 0.0, q).astype(BF16)


def _diff_attend(q, k_bf, v_bf, lam, subln, lam_init):
    q1, q2 = _split_maps(q, 1.0)
    scale = (HEAD_DIM // 2) ** -0.5
    p1 = _softmax_rows(_dot_nt(q1, k_bf) * scale)
    p2 = _softmax_rows(_dot_nt(q2, k_bf) * scale)
    o = _dot((p1 - lam * p2).astype(BF16), v_bf)
    return _rms(o) * subln * (1.0 - lam_init)


def _plain_attend(q_bf, k_bf, v_bf):
    s = _dot_nt(q_bf, k_bf) * (HEAD_DIM ** -0.5)
    p = jnp.exp(s - jnp.max(s, axis=-1, keepdims=True))
    return _dot(p.astype(BF16), v_bf) * (1.0 / jnp.sum(p, axis=-1, keepdims=True))


def _online_softmax_pv(q_bf, k_ref, vaug_ref, chunks):
    nq = q_bf.shape[0]
    m = jnp.full((nq, 1), -jnp.inf, F32)
    acc = jnp.zeros((nq, 2 * HEAD_DIM), F32)
    for start, size in chunks:
        s = _dot_nt(q_bf, k_ref[start:start + size, :])
        m_new = jnp.maximum(m, jnp.max(s, axis=-1, keepdims=True))
        p = jnp.exp2(s - m_new)
        acc = acc * jnp.exp2(m - m_new) +_dot(p.astype(BF16), vaug_ref[start:start + size, :])
        m = m_new
    return acc


def _rope(x, cos, sin, quarter):
    lane = lax.broadcasted_iota(jnp.int32, (1, HEAD_DIM), 1)
    even = (lane // quarter) % 2 == 0
    nxt = pltpu.roll(x, HEAD_DIM - quarter, 1)
    prv = pltpu.roll(x, quarter, 1)
    return x * cos + jnp.where(even, -nxt, prv) * sin


def _rope_tables(t_lat, dim):
    quarter = dim // 4
    inv_freq = ROPE_THETA ** (-jnp.arange(quarter, dtype=F32) / quarter)
    t = jnp.arange(t_lat)
    ang_r = (t // GRID_W).astype(F32)[:, None] * inv_freq
    ang_c = (t % GRID_W).astype(F32)[:, None] * inv_freq
    ang = jnp.concatenate([ang_r, ang_r, ang_c, ang_c] * (HEAD_DIM // dim), axis=-1)
    return jnp.cos(ang), jnp.sin(ang)


def _ctx_attn_kernel(*refs, mode, lam_init, n_q, n_kv):
    q_ref, k_ref, v_ref = refs[:3]
    o_ref, nk_ref, nv_ref = refs[-3:]
    heads = lambda ref, i: ref[:, i * HEAD_DIM:(i + 1) * HEAD_DIM]
    ks, vs = [], []
    for j in range(n_kv):
        k, v = heads(k_ref, j), heads(v_ref, j)
        if mode == "gqa":
            k = _rms(k) * refs[4][...]
        nk_ref[j] = k
        nv_ref[j] = v
        ks.append(k.astype(BF16))
        vs.append(v.astype(BF16))
    if mode == "diff":
        lam = _diff_lambda(*refs[3:7], lam_init)
    for i in range(n_q):
        q = heads(q_ref, i)
        k_bf, v_bf = ks[i % n_kv], vs[i % n_kv]
        if mode == "gqa":
            q = _rms(q) * refs[3][...]
        if mode == "diff":
            o = _diff_attend(q, k_bf, v_bf, lam, refs[7][...], lam_init)
        else:
            o = _plain_attend(q.astype(BF16), k_bf, v_bf)
        o_ref[:, i * HEAD_DIM:(i + 1) * HEAD_DIM] = o.astype(o_ref.dtype)


def ctx_attention(qkv, params, mode, lam_init, n_batch, seq, n_heads, n_kv, m_total):
    grp = n_heads // n_kv
    n_q = grp if grp > 1 else math.gcd(CTX_HEADS_PER_STEP, n_heads)
    n_k = 1 if grp > 1 else n_q
    qw, kw = n_q * HEAD_DIM, n_k * HEAD_DIM
    k_off = n_heads * HEAD_DIM // kw
    v_off = (n_heads + n_kv) * HEAD_DIM // kw
    small = [pl.BlockSpec((1, p.shape[-1]), lambda b, g: (0, 0)) for p in params]
    state = pl.BlockSpec((None, n_k, seq, HEAD_DIM), lambda b, g: (b, g, 0, 0))
    st_shape = jax.ShapeDtypeStruct((n_batch, n_kv, seq, HEAD_DIM), F32)
    return pl.pallas_call(
        functools.partial(_ctx_attn_kernel, mode=mode, lam_init=lam_init, n_q=n_q, n_kv=n_k),
        grid=(n_batch, n_heads // n_q),
        in_specs=[pl.BlockSpec((seq, qw), lambda b, g: (b, g)),
                  pl.BlockSpec((seq, kw), lambda b, g: (b, k_off + g)),
                  pl.BlockSpec((seq, kw), lambda b, g: (b, v_off + g))] + small,
        out_specs=[pl.BlockSpec((seq, qw), lambda b, g: (b, g)), state, state],
        out_shape=[jax.ShapeDtypeStruct((m_total, n_heads * HEAD_DIM), BF16), st_shape, st_shape],
        compiler_params=_cparams("parallel", "parallel"),
        name="ctx_attention_" + mode,
    )(qkv, qkv, qkv, *[p.reshape(1, -1) for p in params])


def _lat_attn_kernel(*refs, mode, lam_init, n_ctx_keys, grp, quarter, chunks):
    (o_in_ref, q_ref, kl_ref, vl_ref, ck_ref, cv_ref,
     cosq_ref, sinq_ref, cosk_ref, sink_ref) = refs[:10]
    o_ref, kall_ref, vaug_ref = refs[-3:]
    del o_in_ref
    params = refs[10:-3]
    h, qi = pl.program_id(1), pl.program_id(2)

    @pl.when(jnp.logical_and(qi == 0, h % grp == 0))
    def _():
        k = kl_ref[...]
        if mode == "gqa":
            k = _rms(k) * params[1][...]
        kall_ref[:n_ctx_keys, :] = ck_ref[...].astype(BF16)
        kall_ref[n_ctx_keys:, :] = _rope(k, cosk_ref[...], sink_ref[...], quarter).astype(BF16)
        vaug_ref[:n_ctx_keys, :HEAD_DIM] = cv_ref[...].astype(BF16)
        vaug_ref[n_ctx_keys:, :HEAD_DIM] = vl_ref[...].astype(BF16)
        vaug_ref[:, HEAD_DIM:] = jnp.ones((vaug_ref.shape[0], HEAD_DIM), BF16)

    q = q_ref[...]
    if mode == "gqa":
        q = _rms(q) * params[0][...]
    q = _rope(q, cosq_ref[...], sinq_ref[...], quarter)
    if mode == "diff":
        lam = _diff_lambda(*params[:4], lam_init)
        q1, q2 = _split_maps(q, (HEAD_DIM // 2) ** -0.5 * LOG2E)
        a1 = _online_softmax_pv(q1, kall_ref, vaug_ref, chunks)
        a2 = _online_softmax_pv(q2, kall_ref, vaug_ref, chunks)
        o = a1[:, :HEAD_DIM] * (1.0 / a1[:, HEAD_DIM:]) - a2[:, :HEAD_DIM] * (lam / a2[:, HEAD_DIM:])
        o = _rms(o) * params[4][...] * (1.0 - lam_init)
    else:
        a = _online_softmax_pv((q * (HEAD_DIM ** -0.5 * LOG2E)).astype(BF16), kall_ref, vaug_ref, chunks)
        o = a[:, :HEAD_DIM] * (1.0 / a[:, HEAD_DIM:])
    o_ref[...] = o.astype(o_ref.dtype)


def latent_attention(o_ctx, qkv, cache_k, cache_v, params, mode, lam_init,
                     n_ctx, n_batch, t_lat, n_heads, n_kv, tq=256):
    grp = n_heads // n_kv
    n_past = cache_k.shape[2]
    tq = _pick(math.gcd(n_ctx, t_lat), tq)
    nq = t_lat // tq
    assert n_ctx % t_lat == 0
    base_q, base_k = n_ctx // tq, n_ctx // t_lat
    dim = HEAD_DIM // 2 if mode == "diff" else HEAD_DIM
    cos, sin = _rope_tables(t_lat, dim)
    kc = _pick(t_lat, KEY_CHUNK)
    chunks = ((0, n_past),) + tuple((n_past + i * kc, kc) for i in range(t_lat // kc))
    qblk = pl.BlockSpec((tq, HEAD_DIM), lambda b, h, i: (base_q + b * nq + i, h))
    kvblk = lambda off: pl.BlockSpec((t_lat, HEAD_DIM), lambda b, h, i: (base_k + b, off + h // grp))
    cblk = pl.BlockSpec((None, None, n_past, HEAD_DIM), lambda b, h, i: (b, h // grp, 0, 0))
    tabq = pl.BlockSpec((tq, HEAD_DIM), lambda b, h, i: (i, 0))
    tabk = pl.BlockSpec((t_lat, HEAD_DIM), lambda b, h, i: (0, 0))
    small = [pl.BlockSpec((1, p.shape[-1]), lambda b, h, i: (0, 0)) for p in params]
    return pl.pallas_call(
        functools.partial(_lat_attn_kernel, mode=mode, lam_init=lam_init,
                          n_ctx_keys=n_past, grp=grp, quarter=dim // 4, chunks=chunks),
        grid=(n_batch, n_heads, nq),
        in_specs=[pl.BlockSpec(memory_space=pl.ANY), qblk, kvblk(n_heads), kvblk(n_heads + n_kv),
                  cblk, cblk, tabq, tabq, tabk, tabk] + small,
        out_specs=qblk,
        out_shape=jax.ShapeDtypeStruct(o_ctx.shape, o_ctx.dtype),
        scratch_shapes=[pltpu.VMEM((n_past + t_lat, HEAD_DIM), BF16),
                        pltpu.VMEM((n_past + t_lat, 2 * HEAD_DIM), BF16)],
        input_output_aliases={0: 0},
        compiler_params=_cparams("parallel", "arbitrary", "arbitrary"),
        name="latent_attention_" + mode,
    )(o_ctx, qkv, qkv, qkv, cache_k, cache_v, cos, sin, cos, sin,
      *[p.reshape(1, -1) for p in params])


def _na_geometry(rows):
    kh = min(NA_ROWS, rows)
    band = min(NA_BAND, rows)
    nblk = rows // NA_QROWS
    qr = np.arange(NA_QROWS)[:, None]
    kr = np.arange(band)[None, :]
    qc = np.arange(GRID_W)[:, None]
    kc = np.arange(GRID_W)[None, :]
    cs = np.clip(qc - NA_COLS // 2, 0, GRID_W - NA_COLS)
    col_ok = (kc >= cs) & (kc < cs + NA_COLS)
    col_idx = np.where(col_ok, kc - qc + NA_COLS - 1, 0)
    starts, pat_ids, pats, keys = [], [], [], []
    for j in range(nblk):
        r0 = j * NA_QROWS
        bs = int(np.clip(r0 - kh // 2, 0, rows - band))
        r = r0 + qr
        rs = np.clip(r - kh // 2, 0, rows - kh)
        key_row = bs + kr
        row_ok = (key_row >= rs) & (key_row < rs + kh)
        row_idx = np.where(row_ok, key_row - r + NA_ROWS - 1, 0)
        key = (bs - r0, tuple((rs - r0).ravel()))
        if key not in keys:
            keys.append(key)
            pats.append((row_ok[:, None, :, None] & col_ok[None, :, None, :], row_idx))
        starts.append(bs)
        pat_ids.append(keys.index(key))
    return starts, pat_ids, pats, col_idx


def _na_bias_tables(rpb, rows):
    starts, pat_ids, pats, col_idx = _na_geometry(rows)
    n_heads = rpb.shape[0]
    col_onehot = jnp.asarray(col_idx[:, :, None] == np.arange(rpb.shape[2]), F32)
    tables = []
    for valid, row_idx in pats:
        by_row = rpb[:, row_idx, :]
        full = jnp.einsum("hqkj,cdj->hqckd", by_row, col_onehot, precision=lax.Precision.HIGHEST)
        full = jnp.where(valid[None], full, MASK_VALUE)
        tables.append(full.reshape(n_heads, valid.shape[0] * GRID_W, valid.shape[2] * GRID_W))
    return starts, pat_ids, jnp.stack(tables)


def _na_attn_kernel(start_ref, pat_ref, o_in_ref, q_ref, kl_ref, vl_ref, ck_ref, cv_ref, bias_ref,
                    o_ref, *, band_keys):
    del pat_ref, o_in_ref
    start = pl.multiple_of(start_ref[pl.program_id(2)] * GRID_W, GRID_W)
    scale = HEAD_DIM ** -0.5
    q = q_ref[...].astype(BF16)
    kb = kl_ref[pl.ds(start, band_keys), :].astype(BF16)
    vb = vl_ref[pl.ds(start, band_keys), :].astype(BF16)
    s_nb = _dot_nt(q, kb) * scale + bias_ref[...]
    s_cx = _dot_nt(q, ck_ref[...].astype(BF16)) * scale
    m = jnp.maximum(jnp.max(s_nb, axis=-1, keepdims=True), jnp.max(s_cx, axis=-1, keepdims=True))
    p_nb = jnp.exp(s_nb - m)
    p_cx = jnp.exp(s_cx - m)
    l = jnp.sum(p_nb, axis=-1, keepdims=True) + jnp.sum(p_cx, axis=-1, keepdims=True)
    o = _dot(p_nb.astype(BF16), vb) + _dot(p_cx.astype(BF16), cv_ref[...].astype(BF16))
    o_ref[...] = (o * (1.0 / l)).astype(o_ref.dtype)


def na_latent_attention(o_ctx, qkv, cache_k, cache_v, rpb, n_ctx, n_batch, t_lat, n_heads):
    rows = t_lat // GRID_W
    assert rows % NA_QROWS == 0 and n_ctx % t_lat == 0
    starts, pat_ids, bias = _na_bias_tables(rpb, rows)
    nblk = len(starts)
    tq = NA_QROWS * GRID_W
    band_keys = min(NA_BAND, rows) * GRID_W
    n_past = cache_k.shape[2]
    base_q, base_k = n_ctx // tq, n_ctx // t_lat
    qblk = pl.BlockSpec((tq, HEAD_DIM), lambda b, h, j, st, pt: (base_q + b * nblk + j, h))
    kvblk = lambda off: pl.BlockSpec((t_lat, HEAD_DIM), lambda b, h, j, st, pt: (base_k + b, off + h))
    cblk = pl.BlockSpec((None, None, n_past, HEAD_DIM), lambda b, h, j, st, pt: (b, h, 0, 0))
    bblk = pl.BlockSpec((None, None, tq, band_keys), lambda b, h, j, st, pt: (pt[j], h, 0, 0))
    return pl.pallas_call(
        functools.partial(_na_attn_kernel, band_keys=band_keys),
        grid_spec=pltpu.PrefetchScalarGridSpec(
            num_scalar_prefetch=2,
            grid=(n_batch, n_heads, nblk),
            in_specs=[pl.BlockSpec(memory_space=pl.ANY), qblk, kvblk(n_heads), kvblk(2 * n_heads),
                      cblk, cblk, bblk],
            out_specs=qblk),
        out_shape=jax.ShapeDtypeStruct(o_ctx.shape, o_ctx.dtype),
        input_output_aliases={2: 0},
        compiler_params=_cparams("parallel", "arbitrary", "arbitrary"),
        name="latent_attention_na",
    )(jnp.asarray(starts, jnp.int32), jnp.asarray(pat_ids, jnp.int32),
      o_ctx, qkv, qkv, qkv, cache_k, cache_v, bias)


def _router_kernel(x_ref, g_ref, sh_ref, sc_ref, rt_ref, h_ref, idx_ref, gate_ref):
    h = _modulate(x_ref[...], g_ref[...], sh_ref[0], sc_ref[0])
    h_ref[...] = h
    logits = lax.dot_general(rt_ref[...], h, (((1,), (1,)), ((), ())),
                             precision=lax.Precision.HIGHEST, preferred_element_type=F32)
    n_exp = logits.shape[0]
    eid = lax.broadcasted_iota(jnp.int32, logits.shape, 0)
    m1 = jnp.max(logits, axis=0, keepdims=True)
    i1 = jnp.min(jnp.where(logits == m1, eid, n_exp), axis=0, keepdims=True)
    rest = jnp.where(eid == i1, -jnp.inf, logits)
    m2 = jnp.max(rest, axis=0, keepdims=True)
    i2 = jnp.min(jnp.where(rest == m2, eid, n_exp), axis=0, keepdims=True)
    e = jnp.exp(m2 - m1)
    idx_ref[0:1, :] = i1
    idx_ref[1:2, :] = i2
    gate_ref[0:1, :] = 1.0 / (1.0 + e)
    gate_ref[1:2, :] = e / (1.0 + e)


def moe_route(x, gain, shift, scale, router, n_ctx, t_lat, tm=512):
    m, d = x.shape
    n_exp = router.shape[1]
    tm = _pick(math.gcd(n_ctx, t_lat), tm)
    grp = lambda i: (_group_of_tile(i, tm, n_ctx, t_lat), 0, 0)
    return pl.pallas_call(
        _router_kernel,
        grid=(m // tm,),
        in_specs=[pl.BlockSpec((tm, d), lambda i: (i, 0)),
                  pl.BlockSpec((1, d), lambda i: (0, 0)),
                  pl.BlockSpec((1, 1, d), grp),
                  pl.BlockSpec((1, 1, d), grp),
                  pl.BlockSpec((n_exp, d), lambda i: (0, 0))],
        out_specs=[pl.BlockSpec((tm, d), lambda i: (i, 0)),
                   pl.BlockSpec((TOP_K, tm), lambda i: (0, i)),
                   pl.BlockSpec((TOP_K, tm), lambda i: (0, i))],
        out_shape=[jax.ShapeDtypeStruct((m, d), F32),
                   jax.ShapeDtypeStruct((TOP_K, m), jnp.int32),
                   jax.ShapeDtypeStruct((TOP_K, m), F32)],
        compiler_params=_cparams("parallel"),
        name="moe_router",
    )(x, gain.reshape(1, d), shift, scale, router.T)


def _gather_rows_kernel(src_ref, h_hbm, o_ref, buf_ref, sem):
    t = pl.program_id(0)
    rows = buf_ref.shape[0]

    def row_copy(r):
        return pltpu.make_async_copy(h_hbm.at[pl.ds(src_ref[t * rows + r], 1)],
                                     buf_ref.at[pl.ds(r, 1)], sem)

    def start(r, c):
        row_copy(r).start()
        return c

    def wait(r, c):
        row_copy(r).wait()
        return c

    lax.fori_loop(0, rows, start, 0)
    lax.fori_loop(0, rows, wait, 0)
    o_ref[...] = buf_ref[...].astype(o_ref.dtype)


def gather_rows(h, src, n_tiles):
    d = h.shape[1]
    return pl.pallas_call(
        _gather_rows_kernel,
        grid_spec=pltpu.PrefetchScalarGridSpec(
            num_scalar_prefetch=1,
            grid=(n_tiles,),
            in_specs=[pl.BlockSpec(memory_space=pl.ANY)],
            out_specs=pl.BlockSpec((MOE_TILE, d), lambda t, s: (t, 0)),
            scratch_shapes=[pltpu.VMEM((MOE_TILE, d), h.dtype), pltpu.SemaphoreType.DMA(())]),
        out_shape=jax.ShapeDtypeStruct((n_tiles * MOE_TILE, d), BF16),
        compiler_params=_cparams("arbitrary"),
        name="moe_gather",
    )(src, h)


def _grouped_matmul_kernel(first_ref, count_ref, x_hbm, *refs, n_w, swiglu, n_tiles):
    w_refs = refs[:n_w]
    o_hbm = refs[n_w]
    wb_refs = refs[n_w + 1:2 * n_w + 1]
    xbuf, obuf, in_sem, out_sem = refs[2 * n_w + 1:]
    j, e = pl.program_id(0), pl.program_id(1)
    first, count = first_ref[e], count_ref[e]
    tile, tn = obuf.shape[1], obuf.shape[2]
    col0 = pl.multiple_of(j * tn, tn)

    def rows(t):
        return pl.ds(pl.multiple_of(t * tile, tile), tile)

    def in_copy(t, slot):
        return pltpu.make_async_copy(x_hbm.at[rows(first + t)], xbuf.at[slot], in_sem.at[slot])

    def out_copy(t, slot):
        return pltpu.make_async_copy(obuf.at[slot], o_hbm.at[rows(t), pl.ds(col0, tn)], out_sem.at[slot])

    @pl.when(count > 0)
    def _():
        in_copy(0, 0).start()
        for w_ref, wb_ref in zip(w_refs, wb_refs):
            _stage_bf16(w_ref, wb_ref)

        def tile_pair(pair, carry):
            for slot in (0, 1):
                t = 2 * pair + slot

                @pl.when(t < count)
                def _():
                    in_copy(t, slot).wait()

                    @pl.when(t + 1 < count)
                    def _():
                        in_copy(t + 1, 1 - slot).start()

                    @pl.when(t >= 2)
                    def _():
                        out_copy(first + t - 2, slot).wait()

                    x = xbuf[slot]
                    if swiglu:
                        a = _dot(x, wb_refs[0][...])
                        b = _dot(x, wb_refs[1][...])
                        obuf[slot] = (a * jax.nn.sigmoid(a) * b).astype(obuf.dtype)
                    else:
                        obuf[slot] = _dot(x, wb_refs[0][...]).astype(obuf.dtype)
                    out_copy(first + t, slot).start()
            return carry

        lax.fori_loop(0, (count + 1) // 2, tile_pair, 0)
        for back in (2, 1):
            @pl.when(count >= back)
            def _():
                t = count - back
                out_copy(first + t, lax.rem(t, 2)).wait()

    @pl.when(e == pl.num_programs(1) - 1)
    def _():
        obuf[0] = jnp.zeros(obuf.shape[1:], obuf.dtype)

        def zero_tile(t, carry):
            cp = out_copy(t, 0)
            cp.start()
            cp.wait()
            return carry

        lax.fori_loop(first + count, n_tiles, zero_tile, 0)


def grouped_matmul(xs, ws, tile_first, tile_count, out_dtype, tn, swiglu=False):
    p, kdim = xs.shape
    n_exp, _, n = ws[0].shape
    tn = _pick(n, tn)
    n_tiles = p // MOE_TILE
    wblk = pl.BlockSpec((None, kdim, tn), lambda j, e, first, count: (e, 0, j))
    return pl.pallas_call(
        functools.partial(_grouped_matmul_kernel, n_w=len(ws), swiglu=swiglu, n_tiles=n_tiles),
        grid_spec=pltpu.PrefetchScalarGridSpec(
            num_scalar_prefetch=2,
            grid=(n // tn, n_exp),
            in_specs=[pl.BlockSpec(memory_space=pl.ANY)] + [wblk for _ in ws],
            out_specs=pl.BlockSpec(memory_space=pl.ANY),
            scratch_shapes=[pltpu.VMEM((kdim, tn), BF16) for _ in ws]
                           + [pltpu.VMEM((2, MOE_TILE, kdim), xs.dtype),
                              pltpu.VMEM((2, MOE_TILE, tn), out_dtype),
                              pltpu.SemaphoreType.DMA((2,)),
                              pltpu.SemaphoreType.DMA((2,))]),
        out_shape=jax.ShapeDtypeStruct((p, n), out_dtype),
        compiler_params=_cparams("arbitrary", "arbitrary"),
        name="moe_expert_swiglu" if swiglu else "moe_expert_down",
    )(tile_first, tile_count, xs, *ws)


def _combine_kernel(p0_ref, p1_ref, x_ref, g_ref, ga_ref, gb_ref, y_hbm, o_ref, ya_ref, yb_ref, sem):
    i = pl.program_id(0)
    rows = ya_ref.shape[0]

    def copies(r):
        return (pltpu.make_async_copy(y_hbm.at[pl.ds(p0_ref[i * rows + r], 1)],
                                      ya_ref.at[pl.ds(r, 1)], sem.at[0]),
                pltpu.make_async_copy(y_hbm.at[pl.ds(p1_ref[i * rows + r], 1)],
                                      yb_ref.at[pl.ds(r, 1)], sem.at[1]))

    def start(r, c):
        for cp in copies(r):
            cp.start()
        return c

    def wait(r, c):
        for cp in copies(r):
            cp.wait()
        return c

    lax.fori_loop(0, rows, start, 0)
    lax.fori_loop(0, rows, wait, 0)
    o_ref[...] = x_ref[...] + g_ref[0] * (ya_ref[...] * ga_ref[...] + yb_ref[...] * gb_ref[...])


def moe_combine(x, gate, y, pos, route_gates, n_ctx, t_lat, tm=256):
    m, d = x.shape
    tm = _pick(math.gcd(n_ctx, t_lat), tm)
    pos0, pos1 = pos[:, 0], pos[:, 1]
    ga, gb = route_gates[:, 0:1], route_gates[:, 1:2]
    return pl.pallas_call(
        _combine_kernel,
        grid_spec=pltpu.PrefetchScalarGridSpec(
            num_scalar_prefetch=2,
            grid=(m // tm,),
            in_specs=[pl.BlockSpec((tm, d), lambda i, a, b: (i, 0)),
                      pl.BlockSpec((1, 1, d), lambda i, a, b: (_group_of_tile(i, tm, n_ctx, t_lat), 0, 0)),
                      pl.BlockSpec((tm, 1), lambda i, a, b: (i, 0)),
                      pl.BlockSpec((tm, 1), lambda i, a, b: (i, 0)),
                      pl.BlockSpec(memory_space=pl.ANY)],
            out_specs=pl.BlockSpec((tm, d), lambda i, a, b: (i, 0)),
            scratch_shapes=[pltpu.VMEM((tm, d), F32), pltpu.VMEM((tm, d), F32),
                            pltpu.SemaphoreType.DMA((2,))]),
        out_shape=jax.ShapeDtypeStruct((m, d), F32),
        compiler_params=_cparams("arbitrary"),
        name="moe_combine",
    )(pos0, pos1, x, gate, ga, gb, y)


def moe_layer(x, gain, shift, scale, gate, router, w_gate, w_up, w_down, n_ctx, t_lat):
    m, _ = x.shape
    n_exp = router.shape[1]
    h, idx, gates = moe_route(x, gain, shift, scale, router, n_ctx, t_lat)
    expert = idx.T.reshape(-1)
    n_assign = expert.shape[0]
    onehot = (expert[:, None] == jnp.arange(n_exp)[None, :]).astype(jnp.int32)
    csum = jnp.cumsum(onehot, axis=0)
    rank = jnp.sum(onehot * (csum - onehot), axis=1)
    counts = csum[-1]
    padded = (counts + MOE_TILE - 1) // MOE_TILE * MOE_TILE
    pad_end = jnp.cumsum(padded)
    dest = jnp.sum(onehot * (pad_end - padded)[None, :], axis=1) + rank
    n_tiles = n_assign // MOE_TILE + n_exp
    token = jnp.arange(n_assign, dtype=jnp.int32) // TOP_K
    src = jnp.zeros((n_tiles * MOE_TILE,), jnp.int32).at[dest].set(token)
    tile_first = ((pad_end - padded) // MOE_TILE).astype(jnp.int32)
    tile_count = (padded // MOE_TILE).astype(jnp.int32)
    pos = dest.reshape(m, TOP_K).astype(jnp.int32)

    xs = gather_rows(h, src, n_tiles)
    u = grouped_matmul(xs, (w_gate, w_up), tile_first, tile_count, BF16, tn=1024, swiglu=True)
    y = grouped_matmul(u, (w_down,), tile_first, tile_count, F32, tn=512)
    return moe_combine(x, gate, y, pos, gates.T, n_ctx, t_lat)


def _final_norm_kernel(x_ref, g_ref, o_ref):
    o_ref[...] = _rms(x_ref[...]) * g_ref[...]


def final_rmsnorm(x, gain, row0, n_rows, tm=512):
    d = x.shape[1]
    tm = _pick(math.gcd(row0, n_rows) if row0 else n_rows, tm)
    blk0 = row0 // tm
    return pl.pallas_call(
        _final_norm_kernel,
        grid=(n_rows // tm,),
        in_specs=[pl.BlockSpec((tm, d), lambda i: (blk0 + i, 0)), pl.BlockSpec((1, d), lambda i: (0, 0))],
        out_specs=pl.BlockSpec((tm, d), lambda i: (i, 0)),
        out_shape=jax.ShapeDtypeStruct((n_rows, d), F32),
        compiler_params=_cparams("parallel"),
        name="final_norm",
    )(x, gain.reshape(1, d))


def kernel(x_prompt, x_sample, c, cache_k_l0, cache_v_l0, cache_k_l1, cache_v_l1, cache_k_l2, cache_v_l2, cache_k_l3, cache_v_l3, c_ctx, final_norm, norm1_l0, norm2_l0, ada_w_l0, ada_b_l0, attn_qkv_l0, attn_out_l0, lambda_q1_l0, lambda_k1_l0, lambda_q2_l0, lambda_k2_l0, subln_l0, ffn_gate_l0, ffn_up_l0, ffn_down_l0, norm1_l1, norm2_l1, ada_w_l1, ada_b_l1, attn_qkv_l1, attn_out_l1, q_norm_l1, k_norm_l1, router_l1, moe_gate_l1, moe_up_l1, moe_down_l1, norm1_l2, norm2_l2, ada_w_l2, ada_b_l2, attn_qkv_l2, attn_out_l2, rpb_l2, ffn_gate_l2, ffn_up_l2, ffn_down_l2, norm1_l3, norm2_l3, ada_w_l3, ada_b_l3, attn_qkv_l3, attn_out_l3, lambda_q1_l3, lambda_k1_l3, lambda_q2_l3, lambda_k2_l3, subln_l3, router_l3, moe_gate_l3, moe_up_l3, moe_down_l3):
    final_gain = final_norm
    mixers = [
        ("diff", attn_qkv_l0, attn_out_l0, (lambda_q1_l0, lambda_k1_l0, lambda_q2_l0, lambda_k2_l0, subln_l0)),
        ("gqa", attn_qkv_l1, attn_out_l1, (q_norm_l1, k_norm_l1)),
        ("na", attn_qkv_l2, attn_out_l2, (rpb_l2,)),
        ("diff", attn_qkv_l3, attn_out_l3, (lambda_q1_l3, lambda_k1_l3, lambda_q2_l3, lambda_k2_l3, subln_l3)),
    ]
    ffns = [
        (ffn_gate_l0, ffn_up_l0, ffn_down_l0),
        (router_l1, moe_gate_l1, moe_up_l1, moe_down_l1),
        (ffn_gate_l2, ffn_up_l2, ffn_down_l2),
        (router_l3, moe_gate_l3, moe_up_l3, moe_down_l3),
    ]
    caches = [(cache_k_l0, cache_v_l0), (cache_k_l1, cache_v_l1),
              (cache_k_l2, cache_v_l2), (cache_k_l3, cache_v_l3)]
    norms = [(norm1_l0, norm2_l0), (norm1_l1, norm2_l1), (norm1_l2, norm2_l2), (norm1_l3, norm2_l3)]
    adas = [(ada_w_l0, ada_b_l0), (ada_w_l1, ada_b_l1), (ada_w_l2, ada_b_l2), (ada_w_l3, ada_b_l3)]

    n_batch, seq, d = x_prompt.shape
    bs, t_lat, _ = x_sample.shape
    n_ctx = n_batch * seq
    m = n_ctx + bs * t_lat
    n_heads = d // HEAD_DIM
    n_groups = 1 + bs
    assert n_groups <= 8
    rows_of = dict(n_ctx=n_ctx, t_lat=t_lat)

    x = jnp.concatenate([x_prompt.reshape(n_ctx, d), x_sample.reshape(bs * t_lat, d)], axis=0)
    cond8 = jnp.zeros((8, d), F32).at[0].set(c_ctx).at[1:n_groups].set(c)

    new_state = []
    for i, (mode, w_qkv, w_out, params) in enumerate(mixers):
        gain1, gain2 = norms[i]
        mod = adaln(cond8, *adas[i])[:n_groups].reshape(n_groups, 6, 1, d)
        sh1, sc1, g1, sh2, sc2, g2 = (mod[:, k] for k in range(6))
        n_kv = caches[i][0].shape[1]
        lam_init = 0.8 - 0.6 * math.exp(-0.3 * i)

        h = modulate_rows(x, gain1, sh1, sc1, n_ctx, t_lat)
        qkv = rw_matmul(h, (w_qkv,), F32, tm=1024, tn=1024, **rows_of)
        ctx_mode = "mha" if mode == "na" else mode
        ctx_params = () if mode == "na" else params
        o, new_k, new_v = ctx_attention(qkv, ctx_params, ctx_mode, lam_init, n_batch, seq, n_heads, n_kv, m)
        if mode == "na":
            o = na_latent_attention(o, qkv, *caches[i], params[0], n_ctx, bs, t_lat, n_heads)
        else:
            o = latent_attention(o, qkv, *caches[i], params, mode, lam_init, n_ctx, bs, t_lat, n_heads, n_kv)
        new_state += [new_k, new_v]
        x = rw_matmul(o, (w_out,), F32, tm=512, tn=1024, epilogue="residual", res=x, gate=g1, **rows_of)

        if i % 2 == 0:
            w_gate, w_up, w_down = ffns[i]
            h = modulate_rows(x, gain2, sh2, sc2, n_ctx, t_lat)
            u = rw_matmul(h, (w_gate, w_up), BF16, tm=1024, tn=512, epilogue="swiglu", **rows_of)
            x = rw_matmul(u, (w_down,), F32, tm=256, tn=512, epilogue="residual", res=x, gate=g2, **rows_of)
        else:
            x = moe_layer(x, gain2, sh2, sc2, g2, *ffns[i], n_ctx, t_lat)

    y_ctx = final_rmsnorm(x, final_gain, 0, n_ctx)
    y_lat = final_rmsnorm(x, final_gain, n_ctx, bs * t_lat)
    return (y_ctx.reshape(n_batch, seq, d), y_lat.reshape(bs, t_lat, d), *new_state)
```

```python
import functools
import math

import numpy as np
import jax
import jax.numpy as jnp
from jax import lax
from jax.experimental import pallas as pl
from jax.experimental.pallas import tpu as pltpu

F32 = jnp.float32
BF16 = jnp.bfloat16

HEAD_DIM = 128
GRID_W = 64
NA_ROWS = 8
NA_COLS = 16
NA_QROWS = 4
NA_BAND = 12
TOP_K = 2
ROPE_THETA = 10000.0
NORM_EPS = 1e-6
MASK_VALUE = -1e30
LOG2E = math.log2(math.e)
VMEM_LIMIT_BYTES = 56 * 2**20
MOE_TILE = 256
CAST_ROWS = 256
KEY_CHUNK = 512
CTX_HEADS_PER_STEP = 4


def _cparams(*sem):
    return pltpu.CompilerParams(dimension_semantics=sem, vmem_limit_bytes=VMEM_LIMIT_BYTES)


def _pick(n, pref, mult=128):
    t = min(pref, n)
    t -= t % mult
    while t > mult and n % t:
        t -= mult
    assert t >= mult and n % t == 0, (n, pref)
    return t


def _dot(a, b):
    return jnp.dot(a, b, preferred_element_type=F32)


def _dot_nt(a, b):
    return lax.dot_general(a, b, (((1,), (1,)), ((), ())), preferred_element_type=F32)


def _rms(x):
    return x * lax.rsqrt(jnp.mean(x * x, axis=-1, keepdims=True) + NORM_EPS)


def _modulate(x, gain, shift, scale):
    return _rms(x) * gain * (1.0 + scale) + shift


def _group_of_tile(i, tm, n_ctx, t_lat):
    row = i * tm
    return jnp.where(row < n_ctx, 0, 1 + (row - n_ctx) // t_lat)


def _stage_bf16(src_ref, dst_ref):
    rows = math.gcd(src_ref.shape[0], CAST_ROWS)

    def body(i, c):
        r = pl.multiple_of(i * rows, rows)
        dst_ref[pl.ds(r, rows), :] = src_ref[pl.ds(r, rows), :].astype(BF16)
        return c

    lax.fori_loop(0, src_ref.shape[0] // rows, body, 0)


def _adaln_kernel(c_ref, w_ref, b_ref, o_ref):
    c = c_ref[...]
    s = c * jax.nn.sigmoid(c)
    o_ref[...] = _dot(s.astype(BF16), w_ref[...].astype(BF16)) + b_ref[...]


def adaln(cond8, w, b):
    d, n = w.shape
    tn = _pick(n, 1024)
    return pl.pallas_call(
        _adaln_kernel,
        grid=(n // tn,),
        in_specs=[pl.BlockSpec((8, d), lambda j: (0, 0)),
                  pl.BlockSpec((d, tn), lambda j: (0, j)),
                  pl.BlockSpec((1, tn), lambda j: (0, j))],
        out_specs=pl.BlockSpec((8, tn), lambda j: (0, j)),
        out_shape=jax.ShapeDtypeStruct((8, n), F32),
        compiler_params=_cparams("parallel"),
        name="adaln",
    )(cond8, w, b.reshape(1, n))


def _modulate_kernel(x_ref, g_ref, sh_ref, sc_ref, o_ref):
    o_ref[...] = _modulate(x_ref[...], g_ref[...], sh_ref[0], sc_ref[0]).astype(o_ref.dtype)


def modulate_rows(x, gain, shift, scale, n_ctx, t_lat, tm=512):
    m, d = x.shape
    tm = _pick(math.gcd(n_ctx, t_lat), tm)
    grp = lambda i: (_group_of_tile(i, tm, n_ctx, t_lat), 0, 0)
    return pl.pallas_call(
        _modulate_kernel,
        grid=(m // tm,),
        in_specs=[pl.BlockSpec((tm, d), lambda i: (i, 0)),
                  pl.BlockSpec((1, d), lambda i: (0, 0)),
                  pl.BlockSpec((1, 1, d), grp),
                  pl.BlockSpec((1, 1, d), grp)],
        out_specs=pl.BlockSpec((tm, d), lambda i: (i, 0)),
        out_shape=jax.ShapeDtypeStruct((m, d), BF16),
        compiler_params=_cparams("parallel"),
        name="modulate",
    )(x, gain.reshape(1, d), shift, scale)


def _rw_matmul_kernel(*refs, n_w, epilogue):
    a_ref = refs[0]
    w_refs = refs[1:1 + n_w]
    extra = refs[1 + n_w:-1 - n_w]
    o_ref = refs[-1 - n_w]
    wb_refs = refs[-n_w:]

    @pl.when(pl.program_id(1) == 0)
    def _():
        for w_ref, wb_ref in zip(w_refs, wb_refs):
            _stage_bf16(w_ref, wb_ref)

    a = a_ref[...]
    if epilogue == "swiglu":
        g = _dot(a, wb_refs[0][...])
        u = _dot(a, wb_refs[1][...])
        o = g * jax.nn.sigmoid(g) * u
    elif epilogue == "residual":
        res_ref, gate_ref = extra
        o = res_ref[...] + gate_ref[0] * _dot(a, wb_refs[0][...])
    else:
        o = _dot(a, wb_refs[0][...])
    o_ref[...] = o.astype(o_ref.dtype)


def rw_matmul(a, ws, out_dtype, tm, tn, epilogue="none", res=None, gate=None, n_ctx=None, t_lat=None,
              w_buffers=2):
    m, kdim = a.shape
    n = ws[0].shape[1]
    tm = _pick(math.gcd(n_ctx, t_lat) if n_ctx else m, tm)
    tn = _pick(n, tn)
    in_specs = [pl.BlockSpec((tm, kdim), lambda j, i: (i, 0))]
    in_specs += [pl.BlockSpec((kdim, tn), lambda j, i: (0, j), pipeline_mode=pl.Buffered(w_buffers)) for _ in ws]
    args = [a, *ws]
    if epilogue == "residual":
        in_specs += [pl.BlockSpec((tm, tn), lambda j, i: (i, j)),
                     pl.BlockSpec((1, 1, tn), lambda j, i: (_group_of_tile(i, tm, n_ctx, t_lat), 0, j))]
        args += [res, gate]
    return pl.pallas_call(
        functools.partial(_rw_matmul_kernel, n_w=len(ws), epilogue=epilogue),
        grid=(n // tn, m // tm),
        in_specs=in_specs,
        out_specs=pl.BlockSpec((tm, tn), lambda j, i: (i, j)),
        out_shape=jax.ShapeDtypeStruct((m, n), out_dtype),
        scratch_shapes=[pltpu.VMEM((kdim, tn), BF16) for _ in ws],
        compiler_params=_cparams("parallel", "arbitrary"),
        name="matmul_" + epilogue,
    )(*args)


def _softmax_rows(s):
    p = jnp.exp(s - jnp.max(s, axis=-1, keepdims=True))
    return p * (1.0 / jnp.sum(p, axis=-1, keepdims=True))


def _diff_lambda(lq1_ref, lk1_ref, lq2_ref, lk2_ref, lam_init):
    a = jnp.sum(lq1_ref[...] * lk1_ref[...], axis=-1, keepdims=True)
    b = jnp.sum(lq2_ref[...] * lk2_ref[...], axis=-1, keepdims=True)
    return jnp.exp(a) - jnp.exp(b) + lam_init


def _split_maps(q, scale):
    first = lax.broadcasted_iota(jnp.int32, (1, HEAD_DIM), 1) < HEAD_DIM // 2
    q = q * scale
    return jnp.where(first, q, 0.0).astype(BF16), jnp.where(first, 0.0, q).astype(BF16)


def _diff_attend(q, k_bf, v_bf, lam, subln, lam_init):
    q1, q2 = _split_maps(q, 1.0)
    scale = (HEAD_DIM // 2) ** -0.5
    p1 = _softmax_rows(_dot_nt(q1, k_bf) * scale)
    p2 = _softmax_rows(_dot_nt(q2, k_bf) * scale)
    o = _dot((p1 - lam * p2).astype(BF16), v_bf)
    return _rms(o) * subln * (1.0 - lam_init)


def _plain_attend(q_bf, k_bf, v_bf):
    s = _dot_nt(q_bf, k_bf) * (HEAD_DIM ** -0.5)
    p = jnp.exp(s - jnp.max(s, axis=-1, keepdims=True))
    return _dot(p.astype(BF16), v_bf) * (1.0 / jnp.sum(p, axis=-1, keepdims=True))


def _online_softmax_pv(q_bf, k_ref, vaug_ref, chunks):
    nq = q_bf.shape[0]
    m = jnp.full((nq, 1), -jnp.inf, F32)
    acc = jnp.zeros((nq, 2 * HEAD_DIM), F32)
    for start, size in chunks:
        s = _dot_nt(q_bf, k_ref[start:start + size, :])
        m_new = jnp.maximum(m, jnp.max(s, axis=-1, keepdims=True))
        p = jnp.exp2(s - m_new)
        acc = acc * jnp.exp2(m - m_new) +_dot(p.astype(BF16), vaug_ref[start:start + size, :])
        m = m_new
    return acc


def _rope(x, cos, sin, quarter):
    lane = lax.broadcasted_iota(jnp.int32, (1, HEAD_DIM), 1)
    even = (lane // quarter) % 2 == 0
    nxt = pltpu.roll(x, HEAD_DIM - quarter, 1)
    prv = pltpu.roll(x, quarter, 1)
    return x * cos + jnp.where(even, -nxt, prv) * sin


def _rope_tables(t_lat, dim):
    quarter = dim // 4
    inv_freq = ROPE_THETA ** (-jnp.arange(quarter, dtype=F32) / quarter)
    t = jnp.arange(t_lat)
    ang_r = (t // GRID_W).astype(F32)[:, None] * inv_freq
    ang_c = (t % GRID_W).astype(F32)[:, None] * inv_freq
    ang = jnp.concatenate([ang_r, ang_r, ang_c, ang_c] * (HEAD_DIM // dim), axis=-1)
    return jnp.cos(ang), jnp.sin(ang)


def _ctx_attn_kernel(*refs, mode, lam_init, n_q, n_kv):
    q_ref, k_ref, v_ref = refs[:3]
    o_ref, nk_ref, nv_ref = refs[-3:]
    heads = lambda ref, i: ref[:, i * HEAD_DIM:(i + 1) * HEAD_DIM]
    ks, vs = [], []
    for j in range(n_kv):
        k, v = heads(k_ref, j), heads(v_ref, j)
        if mode == "gqa":
            k = _rms(k) * refs[4][...]
        nk_ref[j] = k
        nv_ref[j] = v
        ks.append(k.astype(BF16))
        vs.append(v.astype(BF16))
    if mode == "diff":
        lam = _diff_lambda(*refs[3:7], lam_init)
    for i in range(n_q):
        q = heads(q_ref, i)
        k_bf, v_bf = ks[i % n_kv], vs[i % n_kv]
        if mode == "gqa":
            q = _rms(q) * refs[3][...]
        if mode == "diff":
            o = _diff_attend(q, k_bf, v_bf, lam, refs[7][...], lam_init)
        else:
            o = _plain_attend(q.astype(BF16), k_bf, v_bf)
        o_ref[:, i * HEAD_DIM:(i + 1) * HEAD_DIM] = o.astype(o_ref.dtype)


def ctx_attention(qkv, params, mode, lam_init, n_batch, seq, n_heads, n_kv, m_total):
    grp = n_heads // n_kv
    n_q = grp if grp > 1 else math.gcd(CTX_HEADS_PER_STEP, n_heads)
    n_k = 1 if grp > 1 else n_q
    qw, kw = n_q * HEAD_DIM, n_k * HEAD_DIM
    k_off = n_heads * HEAD_DIM // kw
    v_off = (n_heads + n_kv) * HEAD_DIM // kw
    small = [pl.BlockSpec((1, p.shape[-1]), lambda b, g: (0, 0)) for p in params]
    state = pl.BlockSpec((None, n_k, seq, HEAD_DIM), lambda b, g: (b, g, 0, 0))
    st_shape = jax.ShapeDtypeStruct((n_batch, n_kv, seq, HEAD_DIM), F32)
    return pl.pallas_call(
        functools.partial(_ctx_attn_kernel, mode=mode, lam_init=lam_init, n_q=n_q, n_kv=n_k),
        grid=(n_batch, n_heads // n_q),
        in_specs=[pl.BlockSpec((seq, qw), lambda b, g: (b, g)),
                  pl.BlockSpec((seq, kw), lambda b, g: (b, k_off + g)),
                  pl.BlockSpec((seq, kw), lambda b, g: (b, v_off + g))] + small,
        out_specs=[pl.BlockSpec((seq, qw), lambda b, g: (b, g)), state, state],
        out_shape=[jax.ShapeDtypeStruct((m_total, n_heads * HEAD_DIM), BF16), st_shape, st_shape],
        compiler_params=_cparams("parallel", "parallel"),
        name="ctx_attention_" + mode,
    )(qkv, qkv, qkv, *[p.reshape(1, -1) for p in params])


def _lat_attn_kernel(*refs, mode, lam_init, n_ctx_keys, grp, quarter, chunks):
    (o_in_ref, q_ref, kl_ref, vl_ref, ck_ref, cv_ref,
     cosq_ref, sinq_ref, cosk_ref, sink_ref) = refs[:10]
    o_ref, kall_ref, vaug_ref = refs[-3:]
    del o_in_ref
    params = refs[10:-3]
    h, qi = pl.program_id(1), pl.program_id(2)

    @pl.when(jnp.logical_and(qi == 0, h % grp == 0))
    def _():
        k = kl_ref[...]
        if mode == "gqa":
            k = _rms(k) * params[1][...]
        kall_ref[:n_ctx_keys, :] = ck_ref[...].astype(BF16)
        kall_ref[n_ctx_keys:, :] = _rope(k, cosk_ref[...], sink_ref[...], quarter).astype(BF16)
        vaug_ref[:n_ctx_keys, :HEAD_DIM] = cv_ref[...].astype(BF16)
        vaug_ref[n_ctx_keys:, :HEAD_DIM] = vl_ref[...].astype(BF16)
        vaug_ref[:, HEAD_DIM:] = jnp.ones((vaug_ref.shape[0], HEAD_DIM), BF16)

    q = q_ref[...]
    if mode == "gqa":
        q = _rms(q) * params[0][...]
    q = _rope(q, cosq_ref[...], sinq_ref[...], quarter)
    if mode == "diff":
        lam = _diff_lambda(*params[:4], lam_init)
        q1, q2 = _split_maps(q, (HEAD_DIM // 2) ** -0.5 * LOG2E)
        a1 = _online_softmax_pv(q1, kall_ref, vaug_ref, chunks)
        a2 = _online_softmax_pv(q2, kall_ref, vaug_ref, chunks)
        o = a1[:, :HEAD_DIM] * (1.0 / a1[:, HEAD_DIM:]) - a2[:, :HEAD_DIM] * (lam / a2[:, HEAD_DIM:])
        o = _rms(o) * params[4][...] * (1.0 - lam_init)
    else:
        a = _online_softmax_pv((q * (HEAD_DIM ** -0.5 * LOG2E)).astype(BF16), kall_ref, vaug_ref, chunks)
        o = a[:, :HEAD_DIM] * (1.0 / a[:, HEAD_DIM:])
    o_ref[...] = o.astype(o_ref.dtype)


def latent_attention(o_ctx, qkv, cache_k, cache_v, params, mode, lam_init,
                     n_ctx, n_batch, t_lat, n_heads, n_kv, tq=1024):
    grp = n_heads // n_kv
    n_past = cache_k.shape[2]
    tq = _pick(math.gcd(n_ctx, t_lat), tq)
    nq = t_lat // tq
    assert n_ctx % t_lat == 0
    base_q, base_k = n_ctx // tq, n_ctx // t_lat
    dim = HEAD_DIM // 2 if mode == "diff" else HEAD_DIM
    cos, sin = _rope_tables(t_lat, dim)
    kc = _pick(t_lat, KEY_CHUNK)
    chunks = ((0, n_past),) + tuple((n_past + i * kc, kc) for i in range(t_lat // kc))
    qblk = pl.BlockSpec((tq, HEAD_DIM), lambda b, h, i: (base_q + b * nq + i, h))
    kvblk = lambda off: pl.BlockSpec((t_lat, HEAD_DIM), lambda b, h, i: (base_k + b, off + h // grp))
    cblk = pl.BlockSpec((None, None, n_past, HEAD_DIM), lambda b, h, i: (b, h // grp, 0, 0))
    tabq = pl.BlockSpec((tq, HEAD_DIM), lambda b, h, i: (i, 0))
    tabk = pl.BlockSpec((t_lat, HEAD_DIM), lambda b, h, i: (0, 0))
    small = [pl.BlockSpec((1, p.shape[-1]), lambda b, h, i: (0, 0)) for p in params]
    return pl.pallas_call(
        functools.partial(_lat_attn_kernel, mode=mode, lam_init=lam_init,
                          n_ctx_keys=n_past, grp=grp, quarter=dim // 4, chunks=chunks),
        grid=(n_batch, n_heads, nq),
        in_specs=[pl.BlockSpec(memory_space=pl.ANY), qblk, kvblk(n_heads), kvblk(n_heads + n_kv),
                  cblk, cblk, tabq, tabq, tabk, tabk] + small,
        out_specs=qblk,
        out_shape=jax.ShapeDtypeStruct(o_ctx.shape, o_ctx.dtype),
        scratch_shapes=[pltpu.VMEM((n_past + t_lat, HEAD_DIM), BF16),
                        pltpu.VMEM((n_past + t_lat, 2 * HEAD_DIM), BF16)],
        input_output_aliases={0: 0},
        compiler_params=_cparams("parallel", "arbitrary", "arbitrary"),
        name="latent_attention_" + mode,
    )(o_ctx, qkv, qkv, qkv, cache_k, cache_v, cos, sin, cos, sin,
      *[p.reshape(1, -1) for p in params])


def _na_geometry(rows):
    kh = min(NA_ROWS, rows)
    band = min(NA_BAND, rows)
    nblk = rows // NA_QROWS
    qr = np.arange(NA_QROWS)[:, None]
    kr = np.arange(band)[None, :]
    qc = np.arange(GRID_W)[:, None]
    kc = np.arange(GRID_W)[None, :]
    cs = np.clip(qc - NA_COLS // 2, 0, GRID_W - NA_COLS)
    col_ok = (kc >= cs) & (kc < cs + NA_COLS)
    col_idx = np.where(col_ok, kc - qc + NA_COLS - 1, 0)
    starts, pat_ids, pats, keys = [], [], [], []
    for j in range(nblk):
        r0 = j * NA_QROWS
        bs = int(np.clip(r0 - kh // 2, 0, rows - band))
        r = r0 + qr
        rs = np.clip(r - kh // 2, 0, rows - kh)
        key_row = bs + kr
        row_ok = (key_row >= rs) & (key_row < rs + kh)
        row_idx = np.where(row_ok, key_row - r + NA_ROWS - 1, 0)
        key = (bs - r0, tuple((rs - r0).ravel()))
        if key not in keys:
            keys.append(key)
            pats.append((row_ok[:, None, :, None] & col_ok[None, :, None, :], row_idx))
        starts.append(bs)
        pat_ids.append(keys.index(key))
    return starts, pat_ids, pats, col_idx


def _na_bias_tables(rpb, rows):
    starts, pat_ids, pats, col_idx = _na_geometry(rows)
    n_heads = rpb.shape[0]
    col_onehot = jnp.asarray(col_idx[:, :, None] == np.arange(rpb.shape[2]), F32)
    tables = []
    for valid, row_idx in pats:
        by_row = rpb[:, row_idx, :]
        full = jnp.einsum("hqkj,cdj->hqckd", by_row, col_onehot, precision=lax.Precision.HIGHEST)
        full = jnp.where(valid[None], full, MASK_VALUE)
        tables.append(full.reshape(n_heads, valid.shape[0] * GRID_W, valid.shape[2] * GRID_W))
    return starts, pat_ids, jnp.stack(tables)


def _na_attn_kernel(start_ref, pat_ref, o_in_ref, q_ref, kl_ref, vl_ref, ck_ref, cv_ref, bias_ref,
                    o_ref, *, band_keys):
    del pat_ref, o_in_ref
    start = pl.multiple_of(start_ref[pl.program_id(2)] * GRID_W, GRID_W)
    scale = HEAD_DIM ** -0.5
    q = q_ref[...].astype(BF16)
    kb = kl_ref[pl.ds(start, band_keys), :].astype(BF16)
    vb = vl_ref[pl.ds(start, band_keys), :].astype(BF16)
    s_nb = _dot_nt(q, kb) * scale + bias_ref[...]
    s_cx = _dot_nt(q, ck_ref[...].astype(BF16)) * scale
    m = jnp.maximum(jnp.max(s_nb, axis=-1, keepdims=True), jnp.max(s_cx, axis=-1, keepdims=True))
    p_nb = jnp.exp(s_nb - m)
    p_cx = jnp.exp(s_cx - m)
    l = jnp.sum(p_nb, axis=-1, keepdims=True) + jnp.sum(p_cx, axis=-1, keepdims=True)
    o = _dot(p_nb.astype(BF16), vb) + _dot(p_cx.astype(BF16), cv_ref[...].astype(BF16))
    o_ref[...] = (o * (1.0 / l)).astype(o_ref.dtype)


def na_latent_attention(o_ctx, qkv, cache_k, cache_v, rpb, n_ctx, n_batch, t_lat, n_heads):
    rows = t_lat // GRID_W
    assert rows % NA_QROWS == 0 and n_ctx % t_lat == 0
    starts, pat_ids, bias = _na_bias_tables(rpb, rows)
    nblk = len(starts)
    tq = NA_QROWS * GRID_W
    band_keys = min(NA_BAND, rows) * GRID_W
    n_past = cache_k.shape[2]
    base_q, base_k = n_ctx // tq, n_ctx // t_lat
    qblk = pl.BlockSpec((tq, HEAD_DIM), lambda b, h, j, st, pt: (base_q + b * nblk + j, h))
    kvblk = lambda off: pl.BlockSpec((t_lat, HEAD_DIM), lambda b, h, j, st, pt: (base_k + b, off + h))
    cblk = pl.BlockSpec((None, None, n_past, HEAD_DIM), lambda b, h, j, st, pt: (b, h, 0, 0))
    bblk = pl.BlockSpec((None, None, tq, band_keys), lambda b, h, j, st, pt: (pt[j], h, 0, 0))
    return pl.pallas_call(
        functools.partial(_na_attn_kernel, band_keys=band_keys),
        grid_spec=pltpu.PrefetchScalarGridSpec(
            num_scalar_prefetch=2,
            grid=(n_batch, n_heads, nblk),
            in_specs=[pl.BlockSpec(memory_space=pl.ANY), qblk, kvblk(n_heads), kvblk(2 * n_heads),
                      cblk, cblk, bblk],
            out_specs=qblk),
        out_shape=jax.ShapeDtypeStruct(o_ctx.shape, o_ctx.dtype),
        input_output_aliases={2: 0},
        compiler_params=_cparams("parallel", "arbitrary", "arbitrary"),
        name="latent_attention_na",
    )(jnp.asarray(starts, jnp.int32), jnp.asarray(pat_ids, jnp.int32),
      o_ctx, qkv, qkv, qkv, cache_k, cache_v, bias)


def _router_kernel(x_ref, g_ref, sh_ref, sc_ref, rt_ref, h_ref, idx_ref, gate_ref):
    h = _modulate(x_ref[...], g_ref[...], sh_ref[0], sc_ref[0])
    h_ref[...] = h
    logits = lax.dot_general(rt_ref[...], h, (((1,), (1,)), ((), ())),
                             precision=lax.Precision.HIGHEST, preferred_element_type=F32)
    n_exp = logits.shape[0]
    eid = lax.broadcasted_iota(jnp.int32, logits.shape, 0)
    m1 = jnp.max(logits, axis=0, keepdims=True)
    i1 = jnp.min(jnp.where(logits == m1, eid, n_exp), axis=0, keepdims=True)
    rest = jnp.where(eid == i1, -jnp.inf, logits)
    m2 = jnp.max(rest, axis=0, keepdims=True)
    i2 = jnp.min(jnp.where(rest == m2, eid, n_exp), axis=0, keepdims=True)
    e = jnp.exp(m2 - m1)
    idx_ref[0:1, :] = i1
    idx_ref[1:2, :] = i2
    gate_ref[0:1, :] = 1.0 / (1.0 + e)
    gate_ref[1:2, :] = e / (1.0 + e)


def moe_route(x, gain, shift, scale, router, n_ctx, t_lat, tm=512):
    m, d = x.shape
    n_exp = router.shape[1]
    tm = _pick(math.gcd(n_ctx, t_lat), tm)
    grp = lambda i: (_group_of_tile(i, tm, n_ctx, t_lat), 0, 0)
    return pl.pallas_call(
        _router_kernel,
        grid=(m // tm,),
        in_specs=[pl.BlockSpec((tm, d), lambda i: (i, 0)),
                  pl.BlockSpec((1, d), lambda i: (0, 0)),
                  pl.BlockSpec((1, 1, d), grp),
                  pl.BlockSpec((1, 1, d), grp),
                  pl.BlockSpec((n_exp, d), lambda i: (0, 0))],
        out_specs=[pl.BlockSpec((tm, d), lambda i: (i, 0)),
                   pl.BlockSpec((TOP_K, tm), lambda i: (0, i)),
                   pl.BlockSpec((TOP_K, tm), lambda i: (0, i))],
        out_shape=[jax.ShapeDtypeStruct((m, d), F32),
                   jax.ShapeDtypeStruct((TOP_K, m), jnp.int32),
                   jax.ShapeDtypeStruct((TOP_K, m), F32)],
        compiler_params=_cparams("parallel"),
        name="moe_router",
    )(x, gain.reshape(1, d), shift, scale, router.T)


def _gather_rows_kernel(src_ref, h_hbm, o_ref, buf_ref, sem):
    t = pl.program_id(0)
    rows = buf_ref.shape[0]

    def row_copy(r):
        return pltpu.make_async_copy(h_hbm.at[pl.ds(src_ref[t * rows + r], 1)],
                                     buf_ref.at[pl.ds(r, 1)], sem)

    def start(r, c):
        row_copy(r).start()
        return c

    def wait(r, c):
        row_copy(r).wait()
        return c

    lax.fori_loop(0, rows, start, 0)
    lax.fori_loop(0, rows, wait, 0)
    o_ref[...] = buf_ref[...].astype(o_ref.dtype)


def gather_rows(h, src, n_tiles):
    d = h.shape[1]
    return pl.pallas_call(
        _gather_rows_kernel,
        grid_spec=pltpu.PrefetchScalarGridSpec(
            num_scalar_prefetch=1,
            grid=(n_tiles,),
            in_specs=[pl.BlockSpec(memory_space=pl.ANY)],
            out_specs=pl.BlockSpec((MOE_TILE, d), lambda t, s: (t, 0)),
            scratch_shapes=[pltpu.VMEM((MOE_TILE, d), h.dtype), pltpu.SemaphoreType.DMA(())]),
        out_shape=jax.ShapeDtypeStruct((n_tiles * MOE_TILE, d), BF16),
        compiler_params=_cparams("arbitrary"),
        name="moe_gather",
    )(src, h)


def _grouped_matmul_kernel(first_ref, count_ref, x_hbm, *refs, n_w, swiglu, n_tiles, tile):
    w_refs = refs[:n_w]
    o_hbm = refs[n_w]
    wb_refs = refs[n_w + 1:2 * n_w + 1]
    xbuf, obuf, in_sem, out_sem = refs[2 * n_w + 1:]
    j, e = pl.program_id(0), pl.program_id(1)
    first, count = first_ref[e], count_ref[e]
    tn = obuf.shape[2]
    big = obuf.shape[1] // tile
    col0 = pl.multiple_of(j * tn, tn)

    def run_chunks(tile0, n_chunks, ctiles):
        crows = ctiles * tile

        def rows(c):
            return pl.ds(pl.multiple_of((tile0 + c * ctiles) * tile, tile), crows)

        def in_copy(c, slot):
            return pltpu.make_async_copy(x_hbm.at[rows(c)], xbuf.at[slot, pl.ds(0, crows)], in_sem.at[slot])

        def out_copy(c, slot):
            return pltpu.make_async_copy(obuf.at[slot, pl.ds(0, crows)],
                                         o_hbm.at[rows(c), pl.ds(col0, tn)], out_sem.at[slot])

        @pl.when(n_chunks > 0)
        def _():
            in_copy(0, 0).start()

            def chunk_pair(pair, carry):
                for slot in (0, 1):
                    c = 2 * pair + slot

                    @pl.when(c < n_chunks)
                    def _():
                        in_copy(c, slot).wait()

                        @pl.when(c + 1 < n_chunks)
                        def _():
                            in_copy(c + 1, 1 - slot).start()

                        @pl.when(c >= 2)
                        def _():
                            out_copy(c - 2, slot).wait()

                        x = xbuf[slot, :crows]
                        if swiglu:
                            a = _dot(x, wb_refs[0][...])
                            b = _dot(x, wb_refs[1][...])
                            y = a * jax.nn.sigmoid(a) * b
                        else:
                            y = _dot(x, wb_refs[0][...])
                        obuf[slot, :crows] = y.astype(obuf.dtype)
                        out_copy(c, slot).start()
                return carry

            lax.fori_loop(0, (n_chunks + 1) // 2, chunk_pair, 0)
            for back in (2, 1):
                @pl.when(n_chunks >= back)
                def _():
                    c = n_chunks - back
                    out_copy(c, lax.rem(c, 2)).wait()

    @pl.when(count > 0)
    def _():
        for w_ref, wb_ref in zip(w_refs, wb_refs):
            _stage_bf16(w_ref, wb_ref)
        n_big = count // big
        run_chunks(first, n_big, big)
        run_chunks(first + n_big * big, count - n_big * big, 1)

    @pl.when(e == pl.num_programs(1) - 1)
    def _():
        obuf[0, :tile] = jnp.zeros((tile, tn), obuf.dtype)

        def zero_tile(t, carry):
            cp = pltpu.make_async_copy(obuf.at[0, pl.ds(0, tile)],
                                       o_hbm.at[pl.ds(pl.multiple_of(t * tile, tile), tile), pl.ds(col0, tn)],
                                       out_sem.at[0])
            cp.start()
            cp.wait()
            return carry

        lax.fori_loop(first + count, n_tiles, zero_tile, 0)


def grouped_matmul(xs, ws, tile_first, tile_count, out_dtype, tn, chunk_tiles, swiglu=False, w_buffers=2):
    p, kdim = xs.shape
    n_exp, _, n = ws[0].shape
    tn = _pick(n, tn)
    n_tiles = p // MOE_TILE
    chunk_rows = chunk_tiles * MOE_TILE
    wblk = pl.BlockSpec((None, kdim, tn), lambda j, e, first, count: (e, 0, j),
                        pipeline_mode=pl.Buffered(w_buffers))
    return pl.pallas_call(
        functools.partial(_grouped_matmul_kernel, n_w=len(ws), swiglu=swiglu, n_tiles=n_tiles, tile=MOE_TILE),
        grid_spec=pltpu.PrefetchScalarGridSpec(
            num_scalar_prefetch=2,
            grid=(n // tn, n_exp),
            in_specs=[pl.BlockSpec(memory_space=pl.ANY)] + [wblk for _ in ws],
            out_specs=pl.BlockSpec(memory_space=pl.ANY),
            scratch_shapes=[pltpu.VMEM((kdim, tn), BF16) for _ in ws]
                           + [pltpu.VMEM((2, chunk_rows, kdim), xs.dtype),
                              pltpu.VMEM((2, chunk_rows, tn), out_dtype),
                              pltpu.SemaphoreType.DMA((2,)),
                              pltpu.SemaphoreType.DMA((2,))]),
        out_shape=jax.ShapeDtypeStruct((p, n), out_dtype),
        compiler_params=_cparams("arbitrary", "arbitrary"),
        name="moe_expert_swiglu" if swiglu else "moe_expert_down",
    )(tile_first, tile_count, xs, *ws)


def _combine_kernel(p0_ref, p1_ref, x_ref, g_ref, ga_ref, gb_ref, y_hbm, o_ref, ya_ref, yb_ref, sem):
    i = pl.program_id(0)
    rows = ya_ref.shape[0]

    def copies(r):
        return (pltpu.make_async_copy(y_hbm.at[pl.ds(p0_ref[i * rows + r], 1)],
                                      ya_ref.at[pl.ds(r, 1)], sem.at[0]),
                pltpu.make_async_copy(y_hbm.at[pl.ds(p1_ref[i * rows + r], 1)],
                                      yb_ref.at[pl.ds(r, 1)], sem.at[1]))

    def start(r, c):
        for cp in copies(r):
            cp.start()
        return c

    def wait(r, c):
        for cp in copies(r):
            cp.wait()
        return c

    lax.fori_loop(0, rows, start, 0)
    lax.fori_loop(0, rows, wait, 0)
    o_ref[...] = x_ref[...] + g_ref[0] * (ya_ref[...] * ga_ref[...] + yb_ref[...] * gb_ref[...])


def moe_combine(x, gate, y, pos, route_gates, n_ctx, t_lat, tm=256):
    m, d = x.shape
    tm = _pick(math.gcd(n_ctx, t_lat), tm)
    pos0, pos1 = pos[:, 0], pos[:, 1]
    ga, gb = route_gates[:, 0:1], route_gates[:, 1:2]
    return pl.pallas_call(
        _combine_kernel,
        grid_spec=pltpu.PrefetchScalarGridSpec(
            num_scalar_prefetch=2,
            grid=(m // tm,),
            in_specs=[pl.BlockSpec((tm, d), lambda i, a, b: (i, 0)),
                      pl.BlockSpec((1, 1, d), lambda i, a, b: (_group_of_tile(i, tm, n_ctx, t_lat), 0, 0)),
                      pl.BlockSpec((tm, 1), lambda i, a, b: (i, 0)),
                      pl.BlockSpec((tm, 1), lambda i, a, b: (i, 0)),
                      pl.BlockSpec(memory_space=pl.ANY)],
            out_specs=pl.BlockSpec((tm, d), lambda i, a, b: (i, 0)),
            scratch_shapes=[pltpu.VMEM((tm, d), F32), pltpu.VMEM((tm, d), F32),
                            pltpu.SemaphoreType.DMA((2,))]),
        out_shape=jax.ShapeDtypeStruct((m, d), F32),
        compiler_params=_cparams("arbitrary"),
        name="moe_combine",
    )(pos0, pos1, x, gate, ga, gb, y)


def moe_layer(x, gain, shift, scale, gate, router, w_gate, w_up, w_down, n_ctx, t_lat):
    m, _ = x.shape
    n_exp = router.shape[1]
    h, idx, gates = moe_route(x, gain, shift, scale, router, n_ctx, t_lat)
    expert = idx.T.reshape(-1)
    n_assign = expert.shape[0]
    onehot = (expert[:, None] == jnp.arange(n_exp)[None, :]).astype(jnp.int32)
    csum = jnp.cumsum(onehot, axis=0)
    rank = jnp.sum(onehot * (csum - onehot), axis=1)
    counts = csum[-1]
    padded = (counts + MOE_TILE - 1) // MOE_TILE * MOE_TILE
    pad_end = jnp.cumsum(padded)
    dest = jnp.sum(onehot * (pad_end - padded)[None, :], axis=1) + rank
    n_tiles = n_assign // MOE_TILE + n_exp
    token = jnp.arange(n_assign, dtype=jnp.int32) // TOP_K
    src = jnp.zeros((n_tiles * MOE_TILE,), jnp.int32).at[dest].set(token)
    tile_first = ((pad_end - padded) // MOE_TILE).astype(jnp.int32)
    tile_count = (padded // MOE_TILE).astype(jnp.int32)
    pos = dest.reshape(m, TOP_K).astype(jnp.int32)

    xs = gather_rows(h, src, n_tiles)
    u = grouped_matmul(xs, (w_gate, w_up), tile_first, tile_count, BF16, tn=512, chunk_tiles=4, swiglu=True)
    y = grouped_matmul(u, (w_down,), tile_first, tile_count, F32, tn=512, chunk_tiles=2, w_buffers=1)
    return moe_combine(x, gate, y, pos, gates.T, n_ctx, t_lat)


def _final_norm_kernel(x_ref, g_ref, o_ref):
    o_ref[...] = _rms(x_ref[...]) * g_ref[...]


def final_rmsnorm(x, gain, row0, n_rows, tm=512):
    d = x.shape[1]
    tm = _pick(math.gcd(row0, n_rows) if row0 else n_rows, tm)
    blk0 = row0 // tm
    return pl.pallas_call(
        _final_norm_kernel,
        grid=(n_rows // tm,),
        in_specs=[pl.BlockSpec((tm, d), lambda i: (blk0 + i, 0)), pl.BlockSpec((1, d), lambda i: (0, 0))],
        out_specs=pl.BlockSpec((tm, d), lambda i: (i, 0)),
        out_shape=jax.ShapeDtypeStruct((n_rows, d), F32),
        compiler_params=_cparams("parallel"),
        name="final_norm",
    )(x, gain.reshape(1, d))


def kernel(x_prompt, x_sample, c, cache_k_l0, cache_v_l0, cache_k_l1, cache_v_l1, cache_k_l2, cache_v_l2, cache_k_l3, cache_v_l3, c_ctx, final_norm, norm1_l0, norm2_l0, ada_w_l0, ada_b_l0, attn_qkv_l0, attn_out_l0, lambda_q1_l0, lambda_k1_l0, lambda_q2_l0, lambda_k2_l0, subln_l0, ffn_gate_l0, ffn_up_l0, ffn_down_l0, norm1_l1, norm2_l1, ada_w_l1, ada_b_l1, attn_qkv_l1, attn_out_l1, q_norm_l1, k_norm_l1, router_l1, moe_gate_l1, moe_up_l1, moe_down_l1, norm1_l2, norm2_l2, ada_w_l2, ada_b_l2, attn_qkv_l2, attn_out_l2, rpb_l2, ffn_gate_l2, ffn_up_l2, ffn_down_l2, norm1_l3, norm2_l3, ada_w_l3, ada_b_l3, attn_qkv_l3, attn_out_l3, lambda_q1_l3, lambda_k1_l3, lambda_q2_l3, lambda_k2_l3, subln_l3, router_l3, moe_gate_l3, moe_up_l3, moe_down_l3):
    final_gain = final_norm
    mixers = [
        ("diff", attn_qkv_l0, attn_out_l0, (lambda_q1_l0, lambda_k1_l0, lambda_q2_l0, lambda_k2_l0, subln_l0)),
        ("gqa", attn_qkv_l1, attn_out_l1, (q_norm_l1, k_norm_l1)),
        ("na", attn_qkv_l2, attn_out_l2, (rpb_l2,)),
        ("diff", attn_qkv_l3, attn_out_l3, (lambda_q1_l3, lambda_k1_l3, lambda_q2_l3, lambda_k2_l3, subln_l3)),
    ]
    ffns = [
        (ffn_gate_l0, ffn_up_l0, ffn_down_l0),
        (router_l1, moe_gate_l1, moe_up_l1, moe_down_l1),
        (ffn_gate_l2, ffn_up_l2, ffn_down_l2),
        (router_l3, moe_gate_l3, moe_up_l3, moe_down_l3),
    ]
    caches = [(cache_k_l0, cache_v_l0), (cache_k_l1, cache_v_l1),
              (cache_k_l2, cache_v_l2), (cache_k_l3, cache_v_l3)]
    norms = [(norm1_l0, norm2_l0), (norm1_l1, norm2_l1), (norm1_l2, norm2_l2), (norm1_l3, norm2_l3)]
    adas = [(ada_w_l0, ada_b_l0), (ada_w_l1, ada_b_l1), (ada_w_l2, ada_b_l2), (ada_w_l3, ada_b_l3)]

    n_batch, seq, d = x_prompt.shape
    bs, t_lat, _ = x_sample.shape
    n_ctx = n_batch * seq
    m = n_ctx + bs * t_lat
    n_heads = d // HEAD_DIM
    n_groups = 1 + bs
    assert n_groups <= 8
    rows_of = dict(n_ctx=n_ctx, t_lat=t_lat)

    x = jnp.concatenate([x_prompt.reshape(n_ctx, d), x_sample.reshape(bs * t_lat, d)], axis=0)
    cond8 = jnp.zeros((8, d), F32).at[0].set(c_ctx).at[1:n_groups].set(c)

    new_state = []
    for i, (mode, w_qkv, w_out, params) in enumerate(mixers):
        gain1, gain2 = norms[i]
        mod = adaln(cond8, *adas[i])[:n_groups].reshape(n_groups, 6, 1, d)
        sh1, sc1, g1, sh2, sc2, g2 = (mod[:, k] for k in range(6))
        n_kv = caches[i][0].shape[1]
        lam_init = 0.8 - 0.6 * math.exp(-0.3 * i)

        h = modulate_rows(x, gain1, sh1, sc1, n_ctx, t_lat)
        qkv = rw_matmul(h, (w_qkv,), F32, tm=1024, tn=1024, **rows_of)
        ctx_mode = "mha" if mode == "na" else mode
        ctx_params = () if mode == "na" else params
        o, new_k, new_v = ctx_attention(qkv, ctx_params, ctx_mode, lam_init, n_batch, seq, n_heads, n_kv, m)
        if mode == "na":
            o = na_latent_attention(o, qkv, *caches[i], params[0], n_ctx, bs, t_lat, n_heads)
        else:
            o = latent_attention(o, qkv, *caches[i], params, mode, lam_init, n_ctx, bs, t_lat, n_heads, n_kv)
        new_state += [new_k, new_v]
        x = rw_matmul(o, (w_out,), F32, tm=1024, tn=1024, epilogue="residual", res=x, gate=g1, **rows_of)

        if i % 2 == 0:
            w_gate, w_up, w_down = ffns[i]
            h = modulate_rows(x, gain2, sh2, sc2, n_ctx, t_lat)
            u = rw_matmul(h, (w_gate, w_up), BF16, tm=1024, tn=512, epilogue="swiglu", **rows_of)
            x = rw_matmul(u, (w_down,), F32, tm=512, tn=512, epilogue="residual", res=x, gate=g2,
                          w_buffers=1, **rows_of)
        else:
            x = moe_layer(x, gain2, sh2, sc2, g2, *ffns[i], n_ctx, t_lat)

    y_ctx = final_rmsnorm(x, final_gain, 0, n_ctx)
    y_lat = final_rmsnorm(x, final_gain, n_ctx, bs * t_lat)
    return (y_ctx.reshape(n_batch, seq, d), y_lat.reshape(bs, t_lat, d), *new_state)
```

```python
import functools
import math

import numpy as np
import jax
import jax.numpy as jnp
from jax import lax
from jax.experimental import pallas as pl
from jax.experimental.pallas import tpu as pltpu

F32 = jnp.float32
BF16 = jnp.bfloat16

HEAD_DIM = 128
GRID_W = 64
NA_ROWS = 8
NA_COLS = 16
NA_QROWS = 4
NA_BAND = 12
TOP_K = 2
ROPE_THETA = 10000.0
NORM_EPS = 1e-6
MASK_VALUE = -1e30
LOG2E = math.log2(math.e)
VMEM_LIMIT_BYTES = 56 * 2**20
MOE_TILE = 256
CAST_ROWS = 256
KEY_CHUNK = 512
CTX_HEADS_PER_STEP = 4
ROW_DMA_UNROLL = 8


def _cparams(*sem):
    return pltpu.CompilerParams(dimension_semantics=sem, vmem_limit_bytes=VMEM_LIMIT_BYTES)


def _pick(n, pref, mult=128):
    t = min(pref, n)
    t -= t % mult
    while t > mult and n % t:
        t -= mult
    assert t >= mult and n % t == 0, (n, pref)
    return t


def _dot(a, b):
    return jnp.dot(a, b, preferred_element_type=F32)


def _dot_nt(a, b):
    return lax.dot_general(a, b, (((1,), (1,)), ((), ())), preferred_element_type=F32)


def _rms(x):
    return x * lax.rsqrt(jnp.mean(x * x, axis=-1, keepdims=True) + NORM_EPS)


def _modulate(x, gain, shift, scale):
    return _rms(x) * gain * (1.0 + scale) + shift


def _group_of_tile(i, tm, n_ctx, t_lat):
    row = i * tm
    return jnp.where(row < n_ctx, 0, 1 + (row - n_ctx) // t_lat)


def _stage_bf16(src_ref, dst_ref):
    rows = math.gcd(src_ref.shape[0], CAST_ROWS)

    def body(i, c):
        r = pl.multiple_of(i * rows, rows)
        dst_ref[pl.ds(r, rows), :] = src_ref[pl.ds(r, rows), :].astype(BF16)
        return c

    lax.fori_loop(0, src_ref.shape[0] // rows, body, 0)


def _adaln_kernel(c_ref, w_ref, b_ref, o_ref):
    c = c_ref[...]
    s = c * jax.nn.sigmoid(c)
    o_ref[...] = _dot(s.astype(BF16), w_ref[...].astype(BF16)) + b_ref[...]


def adaln(cond8, w, b):
    d, n = w.shape
    tn = _pick(n, 1024)
    return pl.pallas_call(
        _adaln_kernel,
        grid=(n // tn,),
        in_specs=[pl.BlockSpec((8, d), lambda j: (0, 0)),
                  pl.BlockSpec((d, tn), lambda j: (0, j)),
                  pl.BlockSpec((1, tn), lambda j: (0, j))],
        out_specs=pl.BlockSpec((8, tn), lambda j: (0, j)),
        out_shape=jax.ShapeDtypeStruct((8, n), F32),
        compiler_params=_cparams("parallel"),
        name="adaln",
    )(cond8, w, b.reshape(1, n))


def _modulate_kernel(x_ref, g_ref, sh_ref, sc_ref, o_ref):
    o_ref[...] = _modulate(x_ref[...], g_ref[...], sh_ref[0], sc_ref[0]).astype(o_ref.dtype)


def modulate_rows(x, gain, shift, scale, n_ctx, t_lat, tm=512):
    m, d = x.shape
    tm = _pick(math.gcd(n_ctx, t_lat), tm)
    grp = lambda i: (_group_of_tile(i, tm, n_ctx, t_lat), 0, 0)
    return pl.pallas_call(
        _modulate_kernel,
        grid=(m // tm,),
        in_specs=[pl.BlockSpec((tm, d), lambda i: (i, 0)),
                  pl.BlockSpec((1, d), lambda i: (0, 0)),
                  pl.BlockSpec((1, 1, d), grp),
                  pl.BlockSpec((1, 1, d), grp)],
        out_specs=pl.BlockSpec((tm, d), lambda i: (i, 0)),
        out_shape=jax.ShapeDtypeStruct((m, d), BF16),
        compiler_params=_cparams("parallel"),
        name="modulate",
    )(x, gain.reshape(1, d), shift, scale)


def _rw_matmul_kernel(*refs, n_w, epilogue):
    a_ref = refs[0]
    w_refs = refs[1:1 + n_w]
    extra = refs[1 + n_w:-1 - n_w]
    o_ref = refs[-1 - n_w]
    wb_refs = refs[-n_w:]

    @pl.when(pl.program_id(1) == 0)
    def _():
        for w_ref, wb_ref in zip(w_refs, wb_refs):
            _stage_bf16(w_ref, wb_ref)

    a = a_ref[...]
    if epilogue == "swiglu":
        g = _dot(a, wb_refs[0][...])
        u = _dot(a, wb_refs[1][...])
        o = g * jax.nn.sigmoid(g) * u
    elif epilogue == "residual":
        res_ref, gate_ref = extra
        o = res_ref[...] + gate_ref[0] * _dot(a, wb_refs[0][...])
    else:
        o = _dot(a, wb_refs[0][...])
    o_ref[...] = o.astype(o_ref.dtype)


def rw_matmul(a, ws, out_dtype, tm, tn, epilogue="none", res=None, gate=None, n_ctx=None, t_lat=None,
              w_buffers=2):
    m, kdim = a.shape
    n = ws[0].shape[1]
    tm = _pick(math.gcd(n_ctx, t_lat) if n_ctx else m, tm)
    tn = _pick(n, tn)
    in_specs = [pl.BlockSpec((tm, kdim), lambda j, i: (i, 0))]
    in_specs += [pl.BlockSpec((kdim, tn), lambda j, i: (0, j), pipeline_mode=pl.Buffered(w_buffers)) for _ in ws]
    args = [a, *ws]
    if epilogue == "residual":
        in_specs += [pl.BlockSpec((tm, tn), lambda j, i: (i, j)),
                     pl.BlockSpec((1, 1, tn), lambda j, i: (_group_of_tile(i, tm, n_ctx, t_lat), 0, j))]
        args += [res, gate]
    return pl.pallas_call(
        functools.partial(_rw_matmul_kernel, n_w=len(ws), epilogue=epilogue),
        grid=(n // tn, m // tm),
        in_specs=in_specs,
        out_specs=pl.BlockSpec((tm, tn), lambda j, i: (i, j)),
        out_shape=jax.ShapeDtypeStruct((m, n), out_dtype),
        scratch_shapes=[pltpu.VMEM((kdim, tn), BF16) for _ in ws],
        compiler_params=_cparams("parallel", "arbitrary"),
        name="matmul_" + epilogue,
    )(*args)


def _softmax_rows(s):
    p = jnp.exp(s - jnp.max(s, axis=-1, keepdims=True))
    return p * (1.0 / jnp.sum(p, axis=-1, keepdims=True))


def _diff_lambda(lq1_ref, lk1_ref, lq2_ref, lk2_ref, lam_init):
    a = jnp.sum(lq1_ref[...] * lk1_ref[...], axis=-1, keepdims=True)
    b = jnp.sum(lq2_ref[...] * lk2_ref[...], axis=-1, keepdims=True)
    return jnp.exp(a) - jnp.exp(b) + lam_init


def _split_maps(q, scale):
    first = lax.broadcasted_iota(jnp.int32, (1, HEAD_DIM), 1) < HEAD_DIM // 2
    q = q * scale
    return jnp.where(first, q, 0.0).astype(BF16), jnp.where(first, 0.0, q).astype(BF16)


def _diff_attend(q, k_bf, v_bf, lam, subln, lam_init):
    q1, q2 = _split_maps(q, 1.0)
    scale = (HEAD_DIM // 2) ** -0.5
    p1 = _softmax_rows(_dot_nt(q1, k_bf) * scale)
    p2 = _softmax_rows(_dot_nt(q2, k_bf) * scale)
    o = _dot((p1 - lam * p2).astype(BF16), v_bf)
    return _rms(o) * subln * (1.0 - lam_init)


def _plain_attend(q_bf, k_bf, v_bf):
    s = _dot_nt(q_bf, k_bf) * (HEAD_DIM ** -0.5)
    p = jnp.exp(s - jnp.max(s, axis=-1, keepdims=True))
    return _dot(p.astype(BF16), v_bf) * (1.0 / jnp.sum(p, axis=-1, keepdims=True))


def _online_softmax_pv(q_bf, k_ref, vaug_ref, chunks):
    nq = q_bf.shape[0]
    m = jnp.full((nq, 1), -jnp.inf, F32)
    acc = jnp.zeros((nq, 2 * HEAD_DIM), F32)
    for start, size in chunks:
        s = _dot_nt(q_bf, k_ref[start:start + size, :])
        m_new = jnp.maximum(m, jnp.max(s, axis=-1, keepdims=True))
        p = jnp.exp2(s - m_new)
        acc = acc * jnp.exp2(m - m_new) +_dot(p.astype(BF16), vaug_ref[start:start + size, :])
        m = m_new
    return acc


def _rope(x, cos, sin, quarter):
    lane = lax.broadcasted_iota(jnp.int32, (1, HEAD_DIM), 1)
    even = (lane // quarter) % 2 == 0
    nxt = pltpu.roll(x, HEAD_DIM - quarter, 1)
    prv = pltpu.roll(x, quarter, 1)
    return x * cos + jnp.where(even, -nxt, prv) * sin


def _rope_tables(t_lat, dim):
    quarter = dim // 4
    inv_freq = ROPE_THETA ** (-jnp.arange(quarter, dtype=F32) / quarter)
    t = jnp.arange(t_lat)
    ang_r = (t // GRID_W).astype(F32)[:, None] * inv_freq
    ang_c = (t % GRID_W).astype(F32)[:, None] * inv_freq
    ang = jnp.concatenate([ang_r, ang_r, ang_c, ang_c] * (HEAD_DIM // dim), axis=-1)
    return jnp.cos(ang), jnp.sin(ang)


def _ctx_attn_kernel(*refs, mode, lam_init, n_q, n_kv):
    q_ref, k_ref, v_ref = refs[:3]
    o_ref, nk_ref, nv_ref = refs[-3:]
    heads = lambda ref, i: ref[:, i * HEAD_DIM:(i + 1) * HEAD_DIM]
    ks, vs = [], []
    for j in range(n_kv):
        k, v = heads(k_ref, j), heads(v_ref, j)
        if mode == "gqa":
            k = _rms(k) * refs[4][...]
        nk_ref[j] = k
        nv_ref[j] = v
        ks.append(k.astype(BF16))
        vs.append(v.astype(BF16))
    if mode == "diff":
        lam = _diff_lambda(*refs[3:7], lam_init)
    for i in range(n_q):
        q = heads(q_ref, i)
        k_bf, v_bf = ks[i % n_kv], vs[i % n_kv]
        if mode == "gqa":
            q = _rms(q) * refs[3][...]
        if mode == "diff":
            o = _diff_attend(q, k_bf, v_bf, lam, refs[7][...], lam_init)
        else:
            o = _plain_attend(q.astype(BF16), k_bf, v_bf)
        o_ref[:, i * HEAD_DIM:(i + 1) * HEAD_DIM] = o.astype(o_ref.dtype)


def ctx_attention(qkv, params, mode, lam_init, n_batch, seq, n_heads, n_kv, m_total):
    grp = n_heads // n_kv
    n_q = grp if grp > 1 else math.gcd(CTX_HEADS_PER_STEP, n_heads)
    n_k = 1 if grp > 1 else n_q
    qw, kw = n_q * HEAD_DIM, n_k * HEAD_DIM
    k_off = n_heads * HEAD_DIM // kw
    v_off = (n_heads + n_kv) * HEAD_DIM // kw
    small = [pl.BlockSpec((1, p.shape[-1]), lambda b, g: (0, 0)) for p in params]
    state = pl.BlockSpec((None, n_k, seq, HEAD_DIM), lambda b, g: (b, g, 0, 0))
    st_shape = jax.ShapeDtypeStruct((n_batch, n_kv, seq, HEAD_DIM), F32)
    return pl.pallas_call(
        functools.partial(_ctx_attn_kernel, mode=mode, lam_init=lam_init, n_q=n_q, n_kv=n_k),
        grid=(n_batch, n_heads // n_q),
        in_specs=[pl.BlockSpec((seq, qw), lambda b, g: (b, g)),
                  pl.BlockSpec((seq, kw), lambda b, g: (b, k_off + g)),
                  pl.BlockSpec((seq, kw), lambda b, g: (b, v_off + g))] + small,
        out_specs=[pl.BlockSpec((seq, qw), lambda b, g: (b, g)), state, state],
        out_shape=[jax.ShapeDtypeStruct((m_total, n_heads * HEAD_DIM), BF16), st_shape, st_shape],
        compiler_params=_cparams("parallel", "parallel"),
        name="ctx_attention_" + mode,
    )(qkv, qkv, qkv, *[p.reshape(1, -1) for p in params])


def _lat_attn_kernel(*refs, mode, lam_init, n_ctx_keys, grp, quarter, chunks):
    (o_in_ref, q_ref, kl_ref, vl_ref, ck_ref, cv_ref,
     cosq_ref, sinq_ref, cosk_ref, sink_ref) = refs[:10]
    o_ref, kall_ref, vaug_ref = refs[-3:]
    del o_in_ref
    params = refs[10:-3]
    h, qi = pl.program_id(1), pl.program_id(2)

    @pl.when(jnp.logical_and(qi == 0, h % grp == 0))
    def _():
        k = kl_ref[...]
        if mode == "gqa":
            k = _rms(k) * params[1][...]
        kall_ref[:n_ctx_keys, :] = ck_ref[...].astype(BF16)
        kall_ref[n_ctx_keys:, :] = _rope(k, cosk_ref[...], sink_ref[...], quarter).astype(BF16)
        vaug_ref[:n_ctx_keys, :HEAD_DIM] = cv_ref[...].astype(BF16)
        vaug_ref[n_ctx_keys:, :HEAD_DIM] = vl_ref[...].astype(BF16)
        vaug_ref[:, HEAD_DIM:] = jnp.ones((vaug_ref.shape[0], HEAD_DIM), BF16)

    q = q_ref[...]
    if mode == "gqa":
        q = _rms(q) * params[0][...]
    q = _rope(q, cosq_ref[...], sinq_ref[...], quarter)
    if mode == "diff":
        lam = _diff_lambda(*params[:4], lam_init)
        q1, q2 = _split_maps(q, (HEAD_DIM // 2) ** -0.5 * LOG2E)
        a1 = _online_softmax_pv(q1, kall_ref, vaug_ref, chunks)
        a2 = _online_softmax_pv(q2, kall_ref, vaug_ref, chunks)
        o = a1[:, :HEAD_DIM] * (1.0 / a1[:, HEAD_DIM:]) - a2[:, :HEAD_DIM] * (lam / a2[:, HEAD_DIM:])
        o = _rms(o) * params[4][...] * (1.0 - lam_init)
    else:
        a = _online_softmax_pv((q * (HEAD_DIM ** -0.5 * LOG2E)).astype(BF16), kall_ref, vaug_ref, chunks)
        o = a[:, :HEAD_DIM] * (1.0 / a[:, HEAD_DIM:])
    o_ref[...] = o.astype(o_ref.dtype)


def latent_attention(o_ctx, qkv, cache_k, cache_v, params, mode, lam_init,
                     n_ctx, n_batch, t_lat, n_heads, n_kv, tq=1024):
    grp = n_heads // n_kv
    n_past = cache_k.shape[2]
    tq = _pick(math.gcd(n_ctx, t_lat), tq)
    nq = t_lat // tq
    assert n_ctx % t_lat == 0
    base_q, base_k = n_ctx // tq, n_ctx // t_lat
    dim = HEAD_DIM // 2 if mode == "diff" else HEAD_DIM
    cos, sin = _rope_tables(t_lat, dim)
    kc = _pick(t_lat, KEY_CHUNK)
    chunks = ((0, n_past),) + tuple((n_past + i * kc, kc) for i in range(t_lat // kc))
    qblk = pl.BlockSpec((tq, HEAD_DIM), lambda b, h, i: (base_q + b * nq + i, h))
    kvblk = lambda off: pl.BlockSpec((t_lat, HEAD_DIM), lambda b, h, i: (base_k + b, off + h // grp))
    cblk = pl.BlockSpec((None, None, n_past, HEAD_DIM), lambda b, h, i: (b, h // grp, 0, 0))
    tabq = pl.BlockSpec((tq, HEAD_DIM), lambda b, h, i: (i, 0))
    tabk = pl.BlockSpec((t_lat, HEAD_DIM), lambda b, h, i: (0, 0))
    small = [pl.BlockSpec((1, p.shape[-1]), lambda b, h, i: (0, 0)) for p in params]
    return pl.pallas_call(
        functools.partial(_lat_attn_kernel, mode=mode, lam_init=lam_init,
                          n_ctx_keys=n_past, grp=grp, quarter=dim // 4, chunks=chunks),
        grid=(n_batch, n_heads, nq),
        in_specs=[pl.BlockSpec(memory_space=pl.ANY), qblk, kvblk(n_heads), kvblk(n_heads + n_kv),
                  cblk, cblk, tabq, tabq, tabk, tabk] + small,
        out_specs=qblk,
        out_shape=jax.ShapeDtypeStruct(o_ctx.shape, o_ctx.dtype),
        scratch_shapes=[pltpu.VMEM((n_past + t_lat, HEAD_DIM), BF16),
                        pltpu.VMEM((n_past + t_lat, 2 * HEAD_DIM), BF16)],
        input_output_aliases={0: 0},
        compiler_params=_cparams("parallel", "arbitrary", "arbitrary"),
        name="latent_attention_" + mode,
    )(o_ctx, qkv, qkv, qkv, cache_k, cache_v, cos, sin, cos, sin,
      *[p.reshape(1, -1) for p in params])


def _na_geometry(rows):
    kh = min(NA_ROWS, rows)
    band = min(NA_BAND, rows)
    nblk = rows // NA_QROWS
    qr = np.arange(NA_QROWS)[:, None]
    kr = np.arange(band)[None, :]
    qc = np.arange(GRID_W)[:, None]
    kc = np.arange(GRID_W)[None, :]
    cs = np.clip(qc - NA_COLS // 2, 0, GRID_W - NA_COLS)
    col_ok = (kc >= cs) & (kc < cs + NA_COLS)
    col_idx = np.where(col_ok, kc - qc + NA_COLS - 1, 0)
    starts, pat_ids, pats, keys = [], [], [], []
    for j in range(nblk):
        r0 = j * NA_QROWS
        bs = int(np.clip(r0 - kh // 2, 0, rows - band))
        r = r0 + qr
        rs = np.clip(r - kh // 2, 0, rows - kh)
        key_row = bs + kr
        row_ok = (key_row >= rs) & (key_row < rs + kh)
        row_idx = np.where(row_ok, key_row - r + NA_ROWS - 1, 0)
        key = (bs - r0, tuple((rs - r0).ravel()))
        if key not in keys:
            keys.append(key)
            pats.append((row_ok[:, None, :, None] & col_ok[None, :, None, :], row_idx))
        starts.append(bs)
        pat_ids.append(keys.index(key))
    return starts, pat_ids, pats, col_idx


def _na_bias_tables(rpb, rows):
    starts, pat_ids, pats, col_idx = _na_geometry(rows)
    n_heads = rpb.shape[0]
    col_onehot = jnp.asarray(col_idx[:, :, None] == np.arange(rpb.shape[2]), F32)
    tables = []
    for valid, row_idx in pats:
        by_row = rpb[:, row_idx, :]
        full = jnp.einsum("hqkj,cdj->hqckd", by_row, col_onehot, precision=lax.Precision.HIGHEST)
        full = jnp.where(valid[None], full, MASK_VALUE)
        tables.append(full.reshape(n_heads, valid.shape[0] * GRID_W, valid.shape[2] * GRID_W))
    return starts, pat_ids, jnp.stack(tables)


def _na_attn_kernel(start_ref, pat_ref, o_in_ref, q_ref, kl_ref, vl_ref, ck_ref, cv_ref, bias_ref,
                    o_ref, *, band_keys):
    del pat_ref, o_in_ref
    start = pl.multiple_of(start_ref[pl.program_id(2)] * GRID_W, GRID_W)
    scale = HEAD_DIM ** -0.5
    q = q_ref[...].astype(BF16)
    kb = kl_ref[pl.ds(start, band_keys), :].astype(BF16)
    vb = vl_ref[pl.ds(start, band_keys), :].astype(BF16)
    s_nb = _dot_nt(q, kb) * scale + bias_ref[...]
    s_cx = _dot_nt(q, ck_ref[...].astype(BF16)) * scale
    m = jnp.maximum(jnp.max(s_nb, axis=-1, keepdims=True), jnp.max(s_cx, axis=-1, keepdims=True))
    p_nb = jnp.exp(s_nb - m)
    p_cx = jnp.exp(s_cx - m)
    l = jnp.sum(p_nb, axis=-1, keepdims=True) + jnp.sum(p_cx, axis=-1, keepdims=True)
    o = _dot(p_nb.astype(BF16), vb) + _dot(p_cx.astype(BF16), cv_ref[...].astype(BF16))
    o_ref[...] = (o * (1.0 / l)).astype(o_ref.dtype)


def na_latent_attention(o_ctx, qkv, cache_k, cache_v, rpb, n_ctx, n_batch, t_lat, n_heads):
    rows = t_lat // GRID_W
    assert rows % NA_QROWS == 0 and n_ctx % t_lat == 0
    starts, pat_ids, bias = _na_bias_tables(rpb, rows)
    nblk = len(starts)
    tq = NA_QROWS * GRID_W
    band_keys = min(NA_BAND, rows) * GRID_W
    n_past = cache_k.shape[2]
    base_q, base_k = n_ctx // tq, n_ctx // t_lat
    qblk = pl.BlockSpec((tq, HEAD_DIM), lambda b, h, j, st, pt: (base_q + b * nblk + j, h))
    kvblk = lambda off: pl.BlockSpec((t_lat, HEAD_DIM), lambda b, h, j, st, pt: (base_k + b, off + h))
    cblk = pl.BlockSpec((None, None, n_past, HEAD_DIM), lambda b, h, j, st, pt: (b, h, 0, 0))
    bblk = pl.BlockSpec((None, None, tq, band_keys), lambda b, h, j, st, pt: (pt[j], h, 0, 0))
    return pl.pallas_call(
        functools.partial(_na_attn_kernel, band_keys=band_keys),
        grid_spec=pltpu.PrefetchScalarGridSpec(
            num_scalar_prefetch=2,
            grid=(n_batch, n_heads, nblk),
            in_specs=[pl.BlockSpec(memory_space=pl.ANY), qblk, kvblk(n_heads), kvblk(2 * n_heads),
                      cblk, cblk, bblk],
            out_specs=qblk),
        out_shape=jax.ShapeDtypeStruct(o_ctx.shape, o_ctx.dtype),
        input_output_aliases={2: 0},
        compiler_params=_cparams("parallel", "arbitrary", "arbitrary"),
        name="latent_attention_na",
    )(jnp.asarray(starts, jnp.int32), jnp.asarray(pat_ids, jnp.int32),
      o_ctx, qkv, qkv, qkv, cache_k, cache_v, bias)


def _router_kernel(x_ref, g_ref, sh_ref, sc_ref, rt_ref, h_ref, idx_ref, gate_ref):
    h = _modulate(x_ref[...], g_ref[...], sh_ref[0], sc_ref[0])
    h_ref[...] = h
    logits = lax.dot_general(rt_ref[...], h, (((1,), (1,)), ((), ())),
                             precision=lax.Precision.HIGHEST, preferred_element_type=F32)
    n_exp = logits.shape[0]
    eid = lax.broadcasted_iota(jnp.int32, logits.shape, 0)
    m1 = jnp.max(logits, axis=0, keepdims=True)
    i1 = jnp.min(jnp.where(logits == m1, eid, n_exp), axis=0, keepdims=True)
    rest = jnp.where(eid == i1, -jnp.inf, logits)
    m2 = jnp.max(rest, axis=0, keepdims=True)
    i2 = jnp.min(jnp.where(rest == m2, eid, n_exp), axis=0, keepdims=True)
    e = jnp.exp(m2 - m1)
    idx_ref[0:1, :] = i1
    idx_ref[1:2, :] = i2
    gate_ref[0:1, :] = 1.0 / (1.0 + e)
    gate_ref[1:2, :] = e / (1.0 + e)


def moe_route(x, gain, shift, scale, router, n_ctx, t_lat, tm=512):
    m, d = x.shape
    n_exp = router.shape[1]
    tm = _pick(math.gcd(n_ctx, t_lat), tm)
    grp = lambda i: (_group_of_tile(i, tm, n_ctx, t_lat), 0, 0)
    return pl.pallas_call(
        _router_kernel,
        grid=(m // tm,),
        in_specs=[pl.BlockSpec((tm, d), lambda i: (i, 0)),
                  pl.BlockSpec((1, d), lambda i: (0, 0)),
                  pl.BlockSpec((1, 1, d), grp),
                  pl.BlockSpec((1, 1, d), grp),
                  pl.BlockSpec((n_exp, d), lambda i: (0, 0))],
        out_specs=[pl.BlockSpec((tm, d), lambda i: (i, 0)),
                   pl.BlockSpec((TOP_K, tm), lambda i: (0, i)),
                   pl.BlockSpec((TOP_K, tm), lambda i: (0, i))],
        out_shape=[jax.ShapeDtypeStruct((m, d), F32),
                   jax.ShapeDtypeStruct((TOP_K, m), jnp.int32),
                   jax.ShapeDtypeStruct((TOP_K, m), F32)],
        compiler_params=_cparams("parallel"),
        name="moe_router",
    )(x, gain.reshape(1, d), shift, scale, router.T)


def _row_gather(idx_ref, base, src_hbm, dst_ref, sem, wait):
    def body(r, c):
        cp = pltpu.make_async_copy(src_hbm.at[pl.ds(idx_ref[base + r], 1)], dst_ref.at[pl.ds(r, 1)], sem)
        if wait:
            cp.wait()
        else:
            cp.start()
        return c

    lax.fori_loop(0, dst_ref.shape[0], body, 0, unroll=ROW_DMA_UNROLL)


def _gather_rows_kernel(src_ref, h_hbm, o_ref, buf_ref, sem):
    t = pl.program_id(0)
    rows = buf_ref.shape[1]
    slot = lax.rem(t, 2)

    @pl.when(t == 0)
    def _():
        _row_gather(src_ref, 0, h_hbm, buf_ref.at[0], sem.at[0], wait=False)

    @pl.when(t + 1 < pl.num_programs(0))
    def _():
        _row_gather(src_ref, (t + 1) * rows, h_hbm, buf_ref.at[1 - slot], sem.at[1 - slot], wait=False)

    _row_gather(src_ref, t * rows, h_hbm, buf_ref.at[slot], sem.at[slot], wait=True)
    o_ref[...] = buf_ref[slot].astype(o_ref.dtype)


def gather_rows(h, src, n_tiles):
    d = h.shape[1]
    return pl.pallas_call(
        _gather_rows_kernel,
        grid_spec=pltpu.PrefetchScalarGridSpec(
            num_scalar_prefetch=1,
            grid=(n_tiles,),
            in_specs=[pl.BlockSpec(memory_space=pl.ANY)],
            out_specs=pl.BlockSpec((MOE_TILE, d), lambda t, s: (t, 0)),
            scratch_shapes=[pltpu.VMEM((2, MOE_TILE, d), h.dtype), pltpu.SemaphoreType.DMA((2,))]),
        out_shape=jax.ShapeDtypeStruct((n_tiles * MOE_TILE, d), BF16),
        compiler_params=_cparams("arbitrary"),
        name="moe_gather",
    )(src, h)


def _grouped_matmul_kernel(first_ref, count_ref, x_hbm, *refs, n_w, swiglu, n_tiles, tile):
    w_refs = refs[:n_w]
    o_hbm = refs[n_w]
    wb_refs = refs[n_w + 1:2 * n_w + 1]
    xbuf, obuf, in_sem, out_sem = refs[2 * n_w + 1:]
    j, e = pl.program_id(0), pl.program_id(1)
    first, count = first_ref[e], count_ref[e]
    tn = obuf.shape[2]
    big = obuf.shape[1] // tile
    col0 = pl.multiple_of(j * tn, tn)

    def run_chunks(tile0, n_chunks, ctiles):
        crows = ctiles * tile

        def rows(c):
            return pl.ds(pl.multiple_of((tile0 + c * ctiles) * tile, tile), crows)

        def in_copy(c, slot):
            return pltpu.make_async_copy(x_hbm.at[rows(c)], xbuf.at[slot, pl.ds(0, crows)], in_sem.at[slot])

        def out_copy(c, slot):
            return pltpu.make_async_copy(obuf.at[slot, pl.ds(0, crows)],
                                         o_hbm.at[rows(c), pl.ds(col0, tn)], out_sem.at[slot])

        @pl.when(n_chunks > 0)
        def _():
            in_copy(0, 0).start()

            def chunk_pair(pair, carry):
                for slot in (0, 1):
                    c = 2 * pair + slot

                    @pl.when(c < n_chunks)
                    def _():
                        in_copy(c, slot).wait()

                        @pl.when(c + 1 < n_chunks)
                        def _():
                            in_copy(c + 1, 1 - slot).start()

                        @pl.when(c >= 2)
                        def _():
                            out_copy(c - 2, slot).wait()

                        x = xbuf[slot, :crows]
                        if swiglu:
                            a = _dot(x, wb_refs[0][...])
                            b = _dot(x, wb_refs[1][...])
                            y = a * jax.nn.sigmoid(a) * b
                        else:
                            y = _dot(x, wb_refs[0][...])
                        obuf[slot, :crows] = y.astype(obuf.dtype)
                        out_copy(c, slot).start()
                return carry

            lax.fori_loop(0, (n_chunks + 1) // 2, chunk_pair, 0)
            for back in (2, 1):
                @pl.when(n_chunks >= back)
                def _():
                    c = n_chunks - back
                    out_copy(c, lax.rem(c, 2)).wait()

    @pl.when(count > 0)
    def _():
        for w_ref, wb_ref in zip(w_refs, wb_refs):
            _stage_bf16(w_ref, wb_ref)
        n_big = count // big
        run_chunks(first, n_big, big)
        run_chunks(first + n_big * big, count - n_big * big, 1)

    @pl.when(e == pl.num_programs(1) - 1)
    def _():
        obuf[0, :tile] = jnp.zeros((tile, tn), obuf.dtype)

        def zero_tile(t, carry):
            cp = pltpu.make_async_copy(obuf.at[0, pl.ds(0, tile)],
                                       o_hbm.at[pl.ds(pl.multiple_of(t * tile, tile), tile), pl.ds(col0, tn)],
                                       out_sem.at[0])
            cp.start()
            cp.wait()
            return carry

        lax.fori_loop(first + count, n_tiles, zero_tile, 0)


def grouped_matmul(xs, ws, tile_first, tile_count, out_dtype, tn, chunk_tiles, swiglu=False, w_buffers=2):
    p, kdim = xs.shape
    n_exp, _, n = ws[0].shape
    tn = _pick(n, tn)
    n_tiles = p // MOE_TILE
    chunk_rows = chunk_tiles * MOE_TILE
    wblk = pl.BlockSpec((None, kdim, tn), lambda j, e, first, count: (e, 0, j),
                        pipeline_mode=pl.Buffered(w_buffers))
    return pl.pallas_call(
        functools.partial(_grouped_matmul_kernel, n_w=len(ws), swiglu=swiglu, n_tiles=n_tiles, tile=MOE_TILE),
        grid_spec=pltpu.PrefetchScalarGridSpec(
            num_scalar_prefetch=2,
            grid=(n // tn, n_exp),
            in_specs=[pl.BlockSpec(memory_space=pl.ANY)] + [wblk for _ in ws],
            out_specs=pl.BlockSpec(memory_space=pl.ANY),
            scratch_shapes=[pltpu.VMEM((kdim, tn), BF16) for _ in ws]
                           + [pltpu.VMEM((2, chunk_rows, kdim), xs.dtype),
                              pltpu.VMEM((2, chunk_rows, tn), out_dtype),
                              pltpu.SemaphoreType.DMA((2,)),
                              pltpu.SemaphoreType.DMA((2,))]),
        out_shape=jax.ShapeDtypeStruct((p, n), out_dtype),
        compiler_params=_cparams("arbitrary", "arbitrary"),
        name="moe_expert_swiglu" if swiglu else "moe_expert_down",
    )(tile_first, tile_count, xs, *ws)


def _combine_kernel(p0_ref, p1_ref, x_ref, g_ref, ga_ref, gb_ref, y_hbm, o_ref, ya_ref, yb_ref, sem):
    i = pl.program_id(0)
    rows = ya_ref.shape[1]
    slot = lax.rem(i, 2)

    def gather(tile, s, wait):
        _row_gather(p0_ref, tile * rows, y_hbm, ya_ref.at[s], sem.at[0, s], wait)
        _row_gather(p1_ref, tile * rows, y_hbm, yb_ref.at[s], sem.at[1, s], wait)

    @pl.when(i == 0)
    def _():
        gather(0, 0, False)

    @pl.when(i + 1 < pl.num_programs(0))
    def _():
        gather(i + 1, 1 - slot, False)

    gather(i, slot, True)
    o_ref[...] = x_ref[...] + g_ref[0] * (ya_ref[slot] * ga_ref[...] + yb_ref[slot] * gb_ref[...])


def moe_combine(x, gate, y, pos, route_gates, n_ctx, t_lat, tm=256):
    m, d = x.shape
    tm = _pick(math.gcd(n_ctx, t_lat), tm)
    pos0, pos1 = pos[:, 0], pos[:, 1]
    ga, gb = route_gates[:, 0:1], route_gates[:, 1:2]
    return pl.pallas_call(
        _combine_kernel,
        grid_spec=pltpu.PrefetchScalarGridSpec(
            num_scalar_prefetch=2,
            grid=(m // tm,),
            in_specs=[pl.BlockSpec((tm, d), lambda i, a, b: (i, 0)),
                      pl.BlockSpec((1, 1, d), lambda i, a, b: (_group_of_tile(i, tm, n_ctx, t_lat), 0, 0)),
                      pl.BlockSpec((tm, 1), lambda i, a, b: (i, 0)),
                      pl.BlockSpec((tm, 1), lambda i, a, b: (i, 0)),
                      pl.BlockSpec(memory_space=pl.ANY)],
            out_specs=pl.BlockSpec((tm, d), lambda i, a, b: (i, 0)),
            scratch_shapes=[pltpu.VMEM((2, tm, d), F32), pltpu.VMEM((2, tm, d), F32),
                            pltpu.SemaphoreType.DMA((2, 2))]),
        out_shape=jax.ShapeDtypeStruct((m, d), F32),
        compiler_params=_cparams("arbitrary"),
        name="moe_combine",
    )(pos0, pos1, x, gate, ga, gb, y)


def moe_layer(x, gain, shift, scale, gate, router, w_gate, w_up, w_down, n_ctx, t_lat):
    m, _ = x.shape
    n_exp = router.shape[1]
    h, idx, gates = moe_route(x, gain, shift, scale, router, n_ctx, t_lat)
    expert = idx.T.reshape(-1)
    n_assign = expert.shape[0]
    onehot = (expert[:, None] == jnp.arange(n_exp)[None, :]).astype(jnp.int32)
    csum = jnp.cumsum(onehot, axis=0)
    rank = jnp.sum(onehot * (csum - onehot), axis=1)
    counts = csum[-1]
    padded = (counts + MOE_TILE - 1) // MOE_TILE * MOE_TILE
    pad_end = jnp.cumsum(padded)
    dest = jnp.sum(onehot * (pad_end - padded)[None, :], axis=1) + rank
    n_tiles = n_assign // MOE_TILE + n_exp
    token = jnp.arange(n_assign, dtype=jnp.int32) // TOP_K
    src = jnp.zeros((n_tiles * MOE_TILE,), jnp.int32).at[dest].set(token)
    tile_first = ((pad_end - padded) // MOE_TILE).astype(jnp.int32)
    tile_count = (padded // MOE_TILE).astype(jnp.int32)
    pos = dest.reshape(m, TOP_K).astype(jnp.int32)

    xs = gather_rows(h, src, n_tiles)
    u = grouped_matmul(xs, (w_gate, w_up), tile_first, tile_count, BF16, tn=1024, chunk_tiles=2, swiglu=True)
    y = grouped_matmul(u, (w_down,), tile_first, tile_count, F32, tn=512, chunk_tiles=1)
    return moe_combine(x, gate, y, pos, gates.T, n_ctx, t_lat)


def _final_norm_kernel(x_ref, g_ref, o_ref):
    o_ref[...] = _rms(x_ref[...]) * g_ref[...]


def final_rmsnorm(x, gain, row0, n_rows, tm=512):
    d = x.shape[1]
    tm = _pick(math.gcd(row0, n_rows) if row0 else n_rows, tm)
    blk0 = row0 // tm
    return pl.pallas_call(
        _final_norm_kernel,
        grid=(n_rows // tm,),
        in_specs=[pl.BlockSpec((tm, d), lambda i: (blk0 + i, 0)), pl.BlockSpec((1, d), lambda i: (0, 0))],
        out_specs=pl.BlockSpec((tm, d), lambda i: (i, 0)),
        out_shape=jax.ShapeDtypeStruct((n_rows, d), F32),
        compiler_params=_cparams("parallel"),
        name="final_norm",
    )(x, gain.reshape(1, d))


def kernel(x_prompt, x_sample, c, cache_k_l0, cache_v_l0, cache_k_l1, cache_v_l1, cache_k_l2, cache_v_l2, cache_k_l3, cache_v_l3, c_ctx, final_norm, norm1_l0, norm2_l0, ada_w_l0, ada_b_l0, attn_qkv_l0, attn_out_l0, lambda_q1_l0, lambda_k1_l0, lambda_q2_l0, lambda_k2_l0, subln_l0, ffn_gate_l0, ffn_up_l0, ffn_down_l0, norm1_l1, norm2_l1, ada_w_l1, ada_b_l1, attn_qkv_l1, attn_out_l1, q_norm_l1, k_norm_l1, router_l1, moe_gate_l1, moe_up_l1, moe_down_l1, norm1_l2, norm2_l2, ada_w_l2, ada_b_l2, attn_qkv_l2, attn_out_l2, rpb_l2, ffn_gate_l2, ffn_up_l2, ffn_down_l2, norm1_l3, norm2_l3, ada_w_l3, ada_b_l3, attn_qkv_l3, attn_out_l3, lambda_q1_l3, lambda_k1_l3, lambda_q2_l3, lambda_k2_l3, subln_l3, router_l3, moe_gate_l3, moe_up_l3, moe_down_l3):
    final_gain = final_norm
    mixers = [
        ("diff", attn_qkv_l0, attn_out_l0, (lambda_q1_l0, lambda_k1_l0, lambda_q2_l0, lambda_k2_l0, subln_l0)),
        ("gqa", attn_qkv_l1, attn_out_l1, (q_norm_l1, k_norm_l1)),
        ("na", attn_qkv_l2, attn_out_l2, (rpb_l2,)),
        ("diff", attn_qkv_l3, attn_out_l3, (lambda_q1_l3, lambda_k1_l3, lambda_q2_l3, lambda_k2_l3, subln_l3)),
    ]
    ffns = [
        (ffn_gate_l0, ffn_up_l0, ffn_down_l0),
        (router_l1, moe_gate_l1, moe_up_l1, moe_down_l1),
        (ffn_gate_l2, ffn_up_l2, ffn_down_l2),
        (router_l3, moe_gate_l3, moe_up_l3, moe_down_l3),
    ]
    caches = [(cache_k_l0, cache_v_l0), (cache_k_l1, cache_v_l1),
              (cache_k_l2, cache_v_l2), (cache_k_l3, cache_v_l3)]
    norms = [(norm1_l0, norm2_l0), (norm1_l1, norm2_l1), (norm1_l2, norm2_l2), (norm1_l3, norm2_l3)]
    adas = [(ada_w_l0, ada_b_l0), (ada_w_l1, ada_b_l1), (ada_w_l2, ada_b_l2), (ada_w_l3, ada_b_l3)]

    n_batch, seq, d = x_prompt.shape
    bs, t_lat, _ = x_sample.shape
    n_ctx = n_batch * seq
    m = n_ctx + bs * t_lat
    n_heads = d // HEAD_DIM
    n_groups = 1 + bs
    assert n_groups <= 8
    rows_of = dict(n_ctx=n_ctx, t_lat=t_lat)

    x = jnp.concatenate([x_prompt.reshape(n_ctx, d), x_sample.reshape(bs * t_lat, d)], axis=0)
    cond8 = jnp.zeros((8, d), F32).at[0].set(c_ctx).at[1:n_groups].set(c)

    new_state = []
    for i, (mode, w_qkv, w_out, params) in enumerate(mixers):
        gain1, gain2 = norms[i]
        mod = adaln(cond8, *adas[i])[:n_groups].reshape(n_groups, 6, 1, d)
        sh1, sc1, g1, sh2, sc2, g2 = (mod[:, k] for k in range(6))
        n_kv = caches[i][0].shape[1]
        lam_init = 0.8 - 0.6 * math.exp(-0.3 * i)

        h = modulate_rows(x, gain1, sh1, sc1, n_ctx, t_lat)
        qkv = rw_matmul(h, (w_qkv,), F32, tm=1024, tn=1024, **rows_of)
        ctx_mode = "mha" if mode == "na" else mode
        ctx_params = () if mode == "na" else params
        o, new_k, new_v = ctx_attention(qkv, ctx_params, ctx_mode, lam_init, n_batch, seq, n_heads, n_kv, m)
        if mode == "na":
            o = na_latent_attention(o, qkv, *caches[i], params[0], n_ctx, bs, t_lat, n_heads)
        else:
            o = latent_attention(o, qkv, *caches[i], params, mode, lam_init, n_ctx, bs, t_lat, n_heads, n_kv)
        new_state += [new_k, new_v]
        x = rw_matmul(o, (w_out,), F32, tm=1024, tn=1024, epilogue="residual", res=x, gate=g1, **rows_of)

        if i % 2 == 0:
            w_gate, w_up, w_down = ffns[i]
            h = modulate_rows(x, gain2, sh2, sc2, n_ctx, t_lat)
            u = rw_matmul(h, (w_gate, w_up), BF16, tm=1024, tn=512, epilogue="swiglu", **rows_of)
            x = rw_matmul(u, (w_down,), F32, tm=512, tn=512, epilogue="residual", res=x, gate=g2,
                          w_buffers=1, **rows_of)
        else:
            x = moe_layer(x, gain2, sh2, sc2, g2, *ffns[i], n_ctx, t_lat)

    y_ctx = final_rmsnorm(x, final_gain, 0, n_ctx)
    y_lat = final_rmsnorm(x, final_gain, n_ctx, bs * t_lat)
    return (y_ctx.reshape(n_batch, seq, d), y_lat.reshape(bs, t_lat, d), *new_state)
```

```python
import functools
import math

import numpy as np
import jax
import jax.numpy as jnp
from jax import lax
from jax.experimental import pallas as pl
from jax.experimental.pallas import tpu as pltpu

F32 = jnp.float32
BF16 = jnp.bfloat16

HEAD_DIM = 128
GRID_W = 64
NA_ROWS = 8
NA_COLS = 16
NA_QROWS = 4
NA_BAND = 12
TOP_K = 2
ROPE_THETA = 10000.0
NORM_EPS = 1e-6
MASK_VALUE = -1e30
LOG2E = math.log2(math.e)
VMEM_LIMIT_BYTES = 56 * 2**20
MOE_TILE = 256
CAST_ROWS = 256
KEY_CHUNK = 512
CTX_HEADS_PER_STEP = 4
ROW_DMA_UNROLL = 8
ROW_TILE_DMA_PRIORITY = 1


def _cparams(*sem):
    return pltpu.CompilerParams(dimension_semantics=sem, vmem_limit_bytes=VMEM_LIMIT_BYTES)


def _pick(n, pref, mult=128):
    t = min(pref, n)
    t -= t % mult
    while t > mult and n % t:
        t -= mult
    assert t >= mult and n % t == 0, (n, pref)
    return t


def _dot(a, b):
    return jnp.dot(a, b, preferred_element_type=F32)


def _dot_nt(a, b):
    return lax.dot_general(a, b, (((1,), (1,)), ((), ())), preferred_element_type=F32)


def _rms(x):
    return x * lax.rsqrt(jnp.mean(x * x, axis=-1, keepdims=True) + NORM_EPS)


def _modulate(x, gain, shift, scale):
    return _rms(x) * gain * (1.0 + scale) + shift


def _group_of_tile(i, tm, n_ctx, t_lat):
    row = i * tm
    return jnp.where(row < n_ctx, 0, 1 + (row - n_ctx) // t_lat)


def _stage_bf16(src_ref, dst_ref):
    rows = math.gcd(src_ref.shape[0], CAST_ROWS)

    def body(i, c):
        r = pl.multiple_of(i * rows, rows)
        dst_ref[pl.ds(r, rows), :] = src_ref[pl.ds(r, rows), :].astype(BF16)
        return c

    lax.fori_loop(0, src_ref.shape[0] // rows, body, 0)


def _adaln_kernel(c_ref, w_ref, b_ref, o_ref):
    c = c_ref[...]
    s = c * jax.nn.sigmoid(c)
    o_ref[...] = _dot(s.astype(BF16), w_ref[...].astype(BF16)) + b_ref[...]


def adaln(cond8, w, b):
    d, n = w.shape
    tn = _pick(n, 1024)
    return pl.pallas_call(
        _adaln_kernel,
        grid=(n // tn,),
        in_specs=[pl.BlockSpec((8, d), lambda j: (0, 0)),
                  pl.BlockSpec((d, tn), lambda j: (0, j)),
                  pl.BlockSpec((1, tn), lambda j: (0, j))],
        out_specs=pl.BlockSpec((8, tn), lambda j: (0, j)),
        out_shape=jax.ShapeDtypeStruct((8, n), F32),
        compiler_params=_cparams("parallel"),
        name="adaln",
    )(cond8, w, b.reshape(1, n))


def _modulate_kernel(x_ref, g_ref, sh_ref, sc_ref, o_ref):
    o_ref[...] = _modulate(x_ref[...], g_ref[...], sh_ref[0], sc_ref[0]).astype(o_ref.dtype)


def modulate_rows(x, gain, shift, scale, n_ctx, t_lat, tm=512):
    m, d = x.shape
    tm = _pick(math.gcd(n_ctx, t_lat), tm)
    grp = lambda i: (_group_of_tile(i, tm, n_ctx, t_lat), 0, 0)
    return pl.pallas_call(
        _modulate_kernel,
        grid=(m // tm,),
        in_specs=[pl.BlockSpec((tm, d), lambda i: (i, 0)),
                  pl.BlockSpec((1, d), lambda i: (0, 0)),
                  pl.BlockSpec((1, 1, d), grp),
                  pl.BlockSpec((1, 1, d), grp)],
        out_specs=pl.BlockSpec((tm, d), lambda i: (i, 0)),
        out_shape=jax.ShapeDtypeStruct((m, d), BF16),
        compiler_params=_cparams("parallel"),
        name="modulate",
    )(x, gain.reshape(1, d), shift, scale)


def _rw_matmul_kernel(*refs, n_w, epilogue):
    a_ref = refs[0]
    w_refs = refs[1:1 + n_w]
    extra = refs[1 + n_w:-1 - n_w]
    o_ref = refs[-1 - n_w]
    wb_refs = refs[-n_w:]

    @pl.when(pl.program_id(1) == 0)
    def _():
        for w_ref, wb_ref in zip(w_refs, wb_refs):
            _stage_bf16(w_ref, wb_ref)

    a = a_ref[...]
    if epilogue == "swiglu":
        g = _dot(a, wb_refs[0][...])
        u = _dot(a, wb_refs[1][...])
        o = g * jax.nn.sigmoid(g) * u
    elif epilogue == "residual":
        res_ref, gate_ref = extra
        o = res_ref[...] + gate_ref[0] * _dot(a, wb_refs[0][...])
    else:
        o = _dot(a, wb_refs[0][...])
    o_ref[...] = o.astype(o_ref.dtype)


def rw_matmul(a, ws, out_dtype, tm, tn, epilogue="none", res=None, gate=None, n_ctx=None, t_lat=None,
              w_buffers=2):
    m, kdim = a.shape
    n = ws[0].shape[1]
    tm = _pick(math.gcd(n_ctx, t_lat) if n_ctx else m, tm)
    tn = _pick(n, tn)
    in_specs = [pl.BlockSpec((tm, kdim), lambda j, i: (i, 0))]
    in_specs += [pl.BlockSpec((kdim, tn), lambda j, i: (0, j), pipeline_mode=pl.Buffered(w_buffers)) for _ in ws]
    args = [a, *ws]
    if epilogue == "residual":
        in_specs += [pl.BlockSpec((tm, tn), lambda j, i: (i, j)),
                     pl.BlockSpec((1, 1, tn), lambda j, i: (_group_of_tile(i, tm, n_ctx, t_lat), 0, j))]
        args += [res, gate]
    return pl.pallas_call(
        functools.partial(_rw_matmul_kernel, n_w=len(ws), epilogue=epilogue),
        grid=(n // tn, m // tm),
        in_specs=in_specs,
        out_specs=pl.BlockSpec((tm, tn), lambda j, i: (i, j)),
        out_shape=jax.ShapeDtypeStruct((m, n), out_dtype),
        scratch_shapes=[pltpu.VMEM((kdim, tn), BF16) for _ in ws],
        compiler_params=_cparams("parallel", "arbitrary"),
        name="matmul_" + epilogue,
    )(*args)


def _softmax_rows(s):
    p = jnp.exp(s - jnp.max(s, axis=-1, keepdims=True))
    return p * (1.0 / jnp.sum(p, axis=-1, keepdims=True))


def _diff_lambda(lq1_ref, lk1_ref, lq2_ref, lk2_ref, lam_init):
    a = jnp.sum(lq1_ref[...] * lk1_ref[...], axis=-1, keepdims=True)
    b = jnp.sum(lq2_ref[...] * lk2_ref[...], axis=-1, keepdims=True)
    return jnp.exp(a) - jnp.exp(b) + lam_init


def _split_maps(q, scale):
    first = lax.broadcasted_iota(jnp.int32, (1, HEAD_DIM), 1) < HEAD_DIM // 2
    q = q * scale
    return jnp.where(first, q, 0.0).astype(BF16), jnp.where(first, 0.0, q).astype(BF16)


def _diff_attend(q, k_bf, v_bf, lam, subln, lam_init):
    q1, q2 = _split_maps(q, 1.0)
    scale = (HEAD_DIM // 2) ** -0.5
    p1 = _softmax_rows(_dot_nt(q1, k_bf) * scale)
    p2 = _softmax_rows(_dot_nt(q2, k_bf) * scale)
    o = _dot((p1 - lam * p2).astype(BF16), v_bf)
    return _rms(o) * subln * (1.0 - lam_init)


def _plain_attend(q_bf, k_bf, v_bf):
    s = _dot_nt(q_bf, k_bf) * (HEAD_DIM ** -0.5)
    p = jnp.exp(s - jnp.max(s, axis=-1, keepdims=True))
    return _dot(p.astype(BF16), v_bf) * (1.0 / jnp.sum(p, axis=-1, keepdims=True))


def _online_softmax_pv(q_bf, k_ref, vaug_ref, chunks):
    nq = q_bf.shape[0]
    m = jnp.full((nq, 1), -jnp.inf, F32)
    acc = jnp.zeros((nq, 2 * HEAD_DIM), F32)
    for start, size in chunks:
        s = _dot_nt(q_bf, k_ref[start:start + size, :])
        m_new = jnp.maximum(m, jnp.max(s, axis=-1, keepdims=True))
        p = jnp.exp2(s - m_new)
        acc = acc * jnp.exp2(m - m_new) +_dot(p.astype(BF16), vaug_ref[start:start + size, :])
        m = m_new
    return acc


def _rope(x, cos, sin, quarter):
    lane = lax.broadcasted_iota(jnp.int32, (1, HEAD_DIM), 1)
    even = (lane // quarter) % 2 == 0
    nxt = pltpu.roll(x, HEAD_DIM - quarter, 1)
    prv = pltpu.roll(x, quarter, 1)
    return x * cos + jnp.where(even, -nxt, prv) * sin


def _rope_tables(t_lat, dim):
    quarter = dim // 4
    inv_freq = ROPE_THETA ** (-jnp.arange(quarter, dtype=F32) / quarter)
    t = jnp.arange(t_lat)
    ang_r = (t // GRID_W).astype(F32)[:, None] * inv_freq
    ang_c = (t % GRID_W).astype(F32)[:, None] * inv_freq
    ang = jnp.concatenate([ang_r, ang_r, ang_c, ang_c] * (HEAD_DIM // dim), axis=-1)
    return jnp.cos(ang), jnp.sin(ang)


def _ctx_attn_kernel(*refs, mode, lam_init, n_q, n_kv):
    q_ref, k_ref, v_ref = refs[:3]
    o_ref, nk_ref, nv_ref = refs[-3:]
    heads = lambda ref, i: ref[:, i * HEAD_DIM:(i + 1) * HEAD_DIM]
    ks, vs = [], []
    for j in range(n_kv):
        k, v = heads(k_ref, j), heads(v_ref, j)
        if mode == "gqa":
            k = _rms(k) * refs[4][...]
        nk_ref[j] = k
        nv_ref[j] = v
        ks.append(k.astype(BF16))
        vs.append(v.astype(BF16))
    if mode == "diff":
        lam = _diff_lambda(*refs[3:7], lam_init)
    for i in range(n_q):
        q = heads(q_ref, i)
        k_bf, v_bf = ks[i % n_kv], vs[i % n_kv]
        if mode == "gqa":
            q = _rms(q) * refs[3][...]
        if mode == "diff":
            o = _diff_attend(q, k_bf, v_bf, lam, refs[7][...], lam_init)
        else:
            o = _plain_attend(q.astype(BF16), k_bf, v_bf)
        o_ref[:, i * HEAD_DIM:(i + 1) * HEAD_DIM] = o.astype(o_ref.dtype)


def ctx_attention(qkv, params, mode, lam_init, n_batch, seq, n_heads, n_kv, m_total):
    grp = n_heads // n_kv
    n_q = grp if grp > 1 else math.gcd(CTX_HEADS_PER_STEP, n_heads)
    n_k = 1 if grp > 1 else n_q
    qw, kw = n_q * HEAD_DIM, n_k * HEAD_DIM
    k_off = n_heads * HEAD_DIM // kw
    v_off = (n_heads + n_kv) * HEAD_DIM // kw
    small = [pl.BlockSpec((1, p.shape[-1]), lambda b, g: (0, 0)) for p in params]
    state = pl.BlockSpec((None, n_k, seq, HEAD_DIM), lambda b, g: (b, g, 0, 0))
    st_shape = jax.ShapeDtypeStruct((n_batch, n_kv, seq, HEAD_DIM), F32)
    return pl.pallas_call(
        functools.partial(_ctx_attn_kernel, mode=mode, lam_init=lam_init, n_q=n_q, n_kv=n_k),
        grid=(n_batch, n_heads // n_q),
        in_specs=[pl.BlockSpec((seq, qw), lambda b, g: (b, g)),
                  pl.BlockSpec((seq, kw), lambda b, g: (b, k_off + g)),
                  pl.BlockSpec((seq, kw), lambda b, g: (b, v_off + g))] + small,
        out_specs=[pl.BlockSpec((seq, qw), lambda b, g: (b, g)), state, state],
        out_shape=[jax.ShapeDtypeStruct((m_total, n_heads * HEAD_DIM), BF16), st_shape, st_shape],
        compiler_params=_cparams("parallel", "parallel"),
        name="ctx_attention_" + mode,
    )(qkv, qkv, qkv, *[p.reshape(1, -1) for p in params])


def _lat_attn_kernel(*refs, mode, lam_init, n_ctx_keys, grp, quarter, chunks):
    (o_in_ref, q_ref, kl_ref, vl_ref, ck_ref, cv_ref,
     cosq_ref, sinq_ref, cosk_ref, sink_ref) = refs[:10]
    o_ref, kall_ref, vaug_ref = refs[-3:]
    del o_in_ref
    params = refs[10:-3]
    h, qi = pl.program_id(1), pl.program_id(2)

    @pl.when(jnp.logical_and(qi == 0, h % grp == 0))
    def _():
        k = kl_ref[...]
        if mode == "gqa":
            k = _rms(k) * params[1][...]
        kall_ref[:n_ctx_keys, :] = ck_ref[...].astype(BF16)
        kall_ref[n_ctx_keys:, :] = _rope(k, cosk_ref[...], sink_ref[...], quarter).astype(BF16)
        vaug_ref[:n_ctx_keys, :HEAD_DIM] = cv_ref[...].astype(BF16)
        vaug_ref[n_ctx_keys:, :HEAD_DIM] = vl_ref[...].astype(BF16)
        vaug_ref[:, HEAD_DIM:] = jnp.ones((vaug_ref.shape[0], HEAD_DIM), BF16)

    q = q_ref[...]
    if mode == "gqa":
        q = _rms(q) * params[0][...]
    q = _rope(q, cosq_ref[...], sinq_ref[...], quarter)
    if mode == "diff":
        lam = _diff_lambda(*params[:4], lam_init)
        q1, q2 = _split_maps(q, (HEAD_DIM // 2) ** -0.5 * LOG2E)
        a1 = _online_softmax_pv(q1, kall_ref, vaug_ref, chunks)
        a2 = _online_softmax_pv(q2, kall_ref, vaug_ref, chunks)
        o = a1[:, :HEAD_DIM] * (1.0 / a1[:, HEAD_DIM:]) - a2[:, :HEAD_DIM] * (lam / a2[:, HEAD_DIM:])
        o = _rms(o) * params[4][...] * (1.0 - lam_init)
    else:
        a = _online_softmax_pv((q * (HEAD_DIM ** -0.5 * LOG2E)).astype(BF16), kall_ref, vaug_ref, chunks)
        o = a[:, :HEAD_DIM] * (1.0 / a[:, HEAD_DIM:])
    o_ref[...] = o.astype(o_ref.dtype)


def latent_attention(o_ctx, qkv, cache_k, cache_v, params, mode, lam_init,
                     n_ctx, n_batch, t_lat, n_heads, n_kv, tq=1024):
    grp = n_heads // n_kv
    n_past = cache_k.shape[2]
    tq = _pick(math.gcd(n_ctx, t_lat), tq)
    nq = t_lat // tq
    assert n_ctx % t_lat == 0
    base_q, base_k = n_ctx // tq, n_ctx // t_lat
    dim = HEAD_DIM // 2 if mode == "diff" else HEAD_DIM
    cos, sin = _rope_tables(t_lat, dim)
    kc = _pick(t_lat, KEY_CHUNK)
    chunks = ((0, n_past),) + tuple((n_past + i * kc, kc) for i in range(t_lat // kc))
    qblk = pl.BlockSpec((tq, HEAD_DIM), lambda b, h, i: (base_q + b * nq + i, h))
    kvblk = lambda off: pl.BlockSpec((t_lat, HEAD_DIM), lambda b, h, i: (base_k + b, off + h // grp))
    cblk = pl.BlockSpec((None, None, n_past, HEAD_DIM), lambda b, h, i: (b, h // grp, 0, 0))
    tabq = pl.BlockSpec((tq, HEAD_DIM), lambda b, h, i: (i, 0))
    tabk = pl.BlockSpec((t_lat, HEAD_DIM), lambda b, h, i: (0, 0))
    small = [pl.BlockSpec((1, p.shape[-1]), lambda b, h, i: (0, 0)) for p in params]
    return pl.pallas_call(
        functools.partial(_lat_attn_kernel, mode=mode, lam_init=lam_init,
                          n_ctx_keys=n_past, grp=grp, quarter=dim // 4, chunks=chunks),
        grid=(n_batch, n_heads, nq),
        in_specs=[pl.BlockSpec(memory_space=pl.ANY), qblk, kvblk(n_heads), kvblk(n_heads + n_kv),
                  cblk, cblk, tabq, tabq, tabk, tabk] + small,
        out_specs=qblk,
        out_shape=jax.ShapeDtypeStruct(o_ctx.shape, o_ctx.dtype),
        scratch_shapes=[pltpu.VMEM((n_past + t_lat, HEAD_DIM), BF16),
                        pltpu.VMEM((n_past + t_lat, 2 * HEAD_DIM), BF16)],
        input_output_aliases={0: 0},
        compiler_params=_cparams("parallel", "arbitrary", "arbitrary"),
        name="latent_attention_" + mode,
    )(o_ctx, qkv, qkv, qkv, cache_k, cache_v, cos, sin, cos, sin,
      *[p.reshape(1, -1) for p in params])


def _na_geometry(rows):
    kh = min(NA_ROWS, rows)
    band = min(NA_BAND, rows)
    nblk = rows // NA_QROWS
    qr = np.arange(NA_QROWS)[:, None]
    kr = np.arange(band)[None, :]
    qc = np.arange(GRID_W)[:, None]
    kc = np.arange(GRID_W)[None, :]
    cs = np.clip(qc - NA_COLS // 2, 0, GRID_W - NA_COLS)
    col_ok = (kc >= cs) & (kc < cs + NA_COLS)
    col_idx = np.where(col_ok, kc - qc + NA_COLS - 1, 0)
    starts, pat_ids, pats, keys = [], [], [], []
    for j in range(nblk):
        r0 = j * NA_QROWS
        bs = int(np.clip(r0 - kh // 2, 0, rows - band))
        r = r0 + qr
        rs = np.clip(r - kh // 2, 0, rows - kh)
        key_row = bs + kr
        row_ok = (key_row >= rs) & (key_row < rs + kh)
        row_idx = np.where(row_ok, key_row - r + NA_ROWS - 1, 0)
        key = (bs - r0, tuple((rs - r0).ravel()))
        if key not in keys:
            keys.append(key)
            pats.append((row_ok[:, None, :, None] & col_ok[None, :, None, :], row_idx))
        starts.append(bs)
        pat_ids.append(keys.index(key))
    return starts, pat_ids, pats, col_idx


def _na_bias_tables(rpb, rows):
    starts, pat_ids, pats, col_idx = _na_geometry(rows)
    n_heads = rpb.shape[0]
    col_onehot = jnp.asarray(col_idx[:, :, None] == np.arange(rpb.shape[2]), F32)
    tables = []
    for valid, row_idx in pats:
        by_row = rpb[:, row_idx, :]
        full = jnp.einsum("hqkj,cdj->hqckd", by_row, col_onehot, precision=lax.Precision.HIGHEST)
        full = jnp.where(valid[None], full, MASK_VALUE)
        tables.append(full.reshape(n_heads, valid.shape[0] * GRID_W, valid.shape[2] * GRID_W))
    return starts, pat_ids, jnp.stack(tables)


def _na_attn_kernel(start_ref, pat_ref, o_in_ref, q_ref, kl_ref, vl_ref, ck_ref, cv_ref, bias_ref,
                    o_ref, *, band_keys):
    del pat_ref, o_in_ref
    start = pl.multiple_of(start_ref[pl.program_id(2)] * GRID_W, GRID_W)
    scale = HEAD_DIM ** -0.5
    q = q_ref[...].astype(BF16)
    kb = kl_ref[pl.ds(start, band_keys), :].astype(BF16)
    vb = vl_ref[pl.ds(start, band_keys), :].astype(BF16)
    s_nb = _dot_nt(q, kb) * scale + bias_ref[...]
    s_cx = _dot_nt(q, ck_ref[...].astype(BF16)) * scale
    m = jnp.maximum(jnp.max(s_nb, axis=-1, keepdims=True), jnp.max(s_cx, axis=-1, keepdims=True))
    p_nb = jnp.exp(s_nb - m)
    p_cx = jnp.exp(s_cx - m)
    l = jnp.sum(p_nb, axis=-1, keepdims=True) + jnp.sum(p_cx, axis=-1, keepdims=True)
    o = _dot(p_nb.astype(BF16), vb) + _dot(p_cx.astype(BF16), cv_ref[...].astype(BF16))
    o_ref[...] = (o * (1.0 / l)).astype(o_ref.dtype)


def na_latent_attention(o_ctx, qkv, cache_k, cache_v, rpb, n_ctx, n_batch, t_lat, n_heads):
    rows = t_lat // GRID_W
    assert rows % NA_QROWS == 0 and n_ctx % t_lat == 0
    starts, pat_ids, bias = _na_bias_tables(rpb, rows)
    nblk = len(starts)
    tq = NA_QROWS * GRID_W
    band_keys = min(NA_BAND, rows) * GRID_W
    n_past = cache_k.shape[2]
    base_q, base_k = n_ctx // tq, n_ctx // t_lat
    qblk = pl.BlockSpec((tq, HEAD_DIM), lambda b, h, j, st, pt: (base_q + b * nblk + j, h))
    kvblk = lambda off: pl.BlockSpec((t_lat, HEAD_DIM), lambda b, h, j, st, pt: (base_k + b, off + h))
    cblk = pl.BlockSpec((None, None, n_past, HEAD_DIM), lambda b, h, j, st, pt: (b, h, 0, 0))
    bblk = pl.BlockSpec((None, None, tq, band_keys), lambda b, h, j, st, pt: (pt[j], h, 0, 0))
    return pl.pallas_call(
        functools.partial(_na_attn_kernel, band_keys=band_keys),
        grid_spec=pltpu.PrefetchScalarGridSpec(
            num_scalar_prefetch=2,
            grid=(n_batch, n_heads, nblk),
            in_specs=[pl.BlockSpec(memory_space=pl.ANY), qblk, kvblk(n_heads), kvblk(2 * n_heads),
                      cblk, cblk, bblk],
            out_specs=qblk),
        out_shape=jax.ShapeDtypeStruct(o_ctx.shape, o_ctx.dtype),
        input_output_aliases={2: 0},
        compiler_params=_cparams("parallel", "arbitrary", "arbitrary"),
        name="latent_attention_na",
    )(jnp.asarray(starts, jnp.int32), jnp.asarray(pat_ids, jnp.int32),
      o_ctx, qkv, qkv, qkv, cache_k, cache_v, bias)


def _router_kernel(x_ref, g_ref, sh_ref, sc_ref, rt_ref, h_ref, idx_ref, gate_ref):
    h = _modulate(x_ref[...], g_ref[...], sh_ref[0], sc_ref[0])
    h_ref[...] = h
    logits = lax.dot_general(rt_ref[...], h, (((1,), (1,)), ((), ())),
                             precision=lax.Precision.HIGHEST, preferred_element_type=F32)
    n_exp = logits.shape[0]
    eid = lax.broadcasted_iota(jnp.int32, logits.shape, 0)
    m1 = jnp.max(logits, axis=0, keepdims=True)
    i1 = jnp.min(jnp.where(logits == m1, eid, n_exp), axis=0, keepdims=True)
    rest = jnp.where(eid == i1, -jnp.inf, logits)
    m2 = jnp.max(rest, axis=0, keepdims=True)
    i2 = jnp.min(jnp.where(rest == m2, eid, n_exp), axis=0, keepdims=True)
    e = jnp.exp(m2 - m1)
    idx_ref[0:1, :] = i1
    idx_ref[1:2, :] = i2
    gate_ref[0:1, :] = 1.0 / (1.0 + e)
    gate_ref[1:2, :] = e / (1.0 + e)


def moe_route(x, gain, shift, scale, router, n_ctx, t_lat, tm=512):
    m, d = x.shape
    n_exp = router.shape[1]
    tm = _pick(math.gcd(n_ctx, t_lat), tm)
    grp = lambda i: (_group_of_tile(i, tm, n_ctx, t_lat), 0, 0)
    return pl.pallas_call(
        _router_kernel,
        grid=(m // tm,),
        in_specs=[pl.BlockSpec((tm, d), lambda i: (i, 0)),
                  pl.BlockSpec((1, d), lambda i: (0, 0)),
                  pl.BlockSpec((1, 1, d), grp),
                  pl.BlockSpec((1, 1, d), grp),
                  pl.BlockSpec((n_exp, d), lambda i: (0, 0))],
        out_specs=[pl.BlockSpec((tm, d), lambda i: (i, 0)),
                   pl.BlockSpec((TOP_K, tm), lambda i: (0, i)),
                   pl.BlockSpec((TOP_K, tm), lambda i: (0, i))],
        out_shape=[jax.ShapeDtypeStruct((m, d), F32),
                   jax.ShapeDtypeStruct((TOP_K, m), jnp.int32),
                   jax.ShapeDtypeStruct((TOP_K, m), F32)],
        compiler_params=_cparams("parallel"),
        name="moe_router",
    )(x, gain.reshape(1, d), shift, scale, router.T)


def _row_gather(idx_ref, base, src_hbm, dst_ref, sem, wait):
    def body(r, c):
        cp = pltpu.make_async_copy(src_hbm.at[pl.ds(idx_ref[base + r], 1)], dst_ref.at[pl.ds(r, 1)], sem)
        if wait:
            cp.wait()
        else:
            cp.start()
        return c

    lax.fori_loop(0, dst_ref.shape[0], body, 0, unroll=ROW_DMA_UNROLL)


def _gather_rows_kernel(src_ref, h_hbm, o_ref, buf_ref, sem):
    t = pl.program_id(0)
    rows = buf_ref.shape[1]
    slot = lax.rem(t, 2)

    @pl.when(t == 0)
    def _():
        _row_gather(src_ref, 0, h_hbm, buf_ref.at[0], sem.at[0], wait=False)

    @pl.when(t + 1 < pl.num_programs(0))
    def _():
        _row_gather(src_ref, (t + 1) * rows, h_hbm, buf_ref.at[1 - slot], sem.at[1 - slot], wait=False)

    _row_gather(src_ref, t * rows, h_hbm, buf_ref.at[slot], sem.at[slot], wait=True)
    o_ref[...] = buf_ref[slot].astype(o_ref.dtype)


def gather_rows(h, src, n_tiles):
    d = h.shape[1]
    return pl.pallas_call(
        _gather_rows_kernel,
        grid_spec=pltpu.PrefetchScalarGridSpec(
            num_scalar_prefetch=1,
            grid=(n_tiles,),
            in_specs=[pl.BlockSpec(memory_space=pl.ANY)],
            out_specs=pl.BlockSpec((MOE_TILE, d), lambda t, s: (t, 0)),
            scratch_shapes=[pltpu.VMEM((2, MOE_TILE, d), h.dtype), pltpu.SemaphoreType.DMA((2,))]),
        out_shape=jax.ShapeDtypeStruct((n_tiles * MOE_TILE, d), BF16),
        compiler_params=_cparams("arbitrary"),
        name="moe_gather",
    )(src, h)


def _grouped_matmul_kernel(first_ref, count_ref, x_hbm, *refs, n_w, swiglu, n_tiles, tile):
    w_refs = refs[:n_w]
    o_hbm = refs[n_w]
    wb_refs = refs[n_w + 1:2 * n_w + 1]
    xbuf, obuf, in_sem, out_sem = refs[2 * n_w + 1:]
    j, e = pl.program_id(0), pl.program_id(1)
    first, count = first_ref[e], count_ref[e]
    tn = obuf.shape[2]
    big = obuf.shape[1] // tile
    col0 = pl.multiple_of(j * tn, tn)

    def run_chunks(tile0, n_chunks, ctiles):
        crows = ctiles * tile

        def rows(c):
            return pl.ds(pl.multiple_of((tile0 + c * ctiles) * tile, tile), crows)

        def in_copy(c, slot):
            return pltpu.make_async_copy(x_hbm.at[rows(c)], xbuf.at[slot, pl.ds(0, crows)], in_sem.at[slot])

        def out_copy(c, slot):
            return pltpu.make_async_copy(obuf.at[slot, pl.ds(0, crows)],
                                         o_hbm.at[rows(c), pl.ds(col0, tn)], out_sem.at[slot])

        @pl.when(n_chunks > 0)
        def _():
            in_copy(0, 0).start(priority=ROW_TILE_DMA_PRIORITY)

            def chunk_pair(pair, carry):
                for slot in (0, 1):
                    c = 2 * pair + slot

                    @pl.when(c < n_chunks)
                    def _():
                        in_copy(c, slot).wait()

                        @pl.when(c + 1 < n_chunks)
                        def _():
                            in_copy(c + 1, 1 - slot).start(priority=ROW_TILE_DMA_PRIORITY)

                        @pl.when(c >= 2)
                        def _():
                            out_copy(c - 2, slot).wait()

                        x = xbuf[slot, :crows]
                        if swiglu:
                            a = _dot(x, wb_refs[0][...])
                            b = _dot(x, wb_refs[1][...])
                            y = a * jax.nn.sigmoid(a) * b
                        else:
                            y = _dot(x, wb_refs[0][...])
                        obuf[slot, :crows] = y.astype(obuf.dtype)
                        out_copy(c, slot).start(priority=ROW_TILE_DMA_PRIORITY)
                return carry

            lax.fori_loop(0, (n_chunks + 1) // 2, chunk_pair, 0)
            for back in (2, 1):
                @pl.when(n_chunks >= back)
                def _():
                    c = n_chunks - back
                    out_copy(c, lax.rem(c, 2)).wait()

    @pl.when(count > 0)
    def _():
        for w_ref, wb_ref in zip(w_refs, wb_refs):
            _stage_bf16(w_ref, wb_ref)
        n_big = count // big
        run_chunks(first, n_big, big)
        run_chunks(first + n_big * big, count - n_big * big, 1)

    @pl.when(e == pl.num_programs(1) - 1)
    def _():
        obuf[0, :tile] = jnp.zeros((tile, tn), obuf.dtype)

        def zero_tile(t, carry):
            cp = pltpu.make_async_copy(obuf.at[0, pl.ds(0, tile)],
                                       o_hbm.at[pl.ds(pl.multiple_of(t * tile, tile), tile), pl.ds(col0, tn)],
                                       out_sem.at[0])
            cp.start()
            cp.wait()
            return carry

        lax.fori_loop(first + count, n_tiles, zero_tile, 0)


def grouped_matmul(xs, ws, tile_first, tile_count, out_dtype, tn, chunk_tiles, swiglu=False, w_buffers=2):
    p, kdim = xs.shape
    n_exp, _, n = ws[0].shape
    tn = _pick(n, tn)
    n_tiles = p // MOE_TILE
    chunk_rows = chunk_tiles * MOE_TILE
    wblk = pl.BlockSpec((None, kdim, tn), lambda j, e, first, count: (e, 0, j),
                        pipeline_mode=pl.Buffered(w_buffers))
    return pl.pallas_call(
        functools.partial(_grouped_matmul_kernel, n_w=len(ws), swiglu=swiglu, n_tiles=n_tiles, tile=MOE_TILE),
        grid_spec=pltpu.PrefetchScalarGridSpec(
            num_scalar_prefetch=2,
            grid=(n // tn, n_exp),
            in_specs=[pl.BlockSpec(memory_space=pl.ANY)] + [wblk for _ in ws],
            out_specs=pl.BlockSpec(memory_space=pl.ANY),
            scratch_shapes=[pltpu.VMEM((kdim, tn), BF16) for _ in ws]
                           + [pltpu.VMEM((2, chunk_rows, kdim), xs.dtype),
                              pltpu.VMEM((2, chunk_rows, tn), out_dtype),
                              pltpu.SemaphoreType.DMA((2,)),
                              pltpu.SemaphoreType.DMA((2,))]),
        out_shape=jax.ShapeDtypeStruct((p, n), out_dtype),
        compiler_params=_cparams("arbitrary", "arbitrary"),
        name="moe_expert_swiglu" if swiglu else "moe_expert_down",
    )(tile_first, tile_count, xs, *ws)


def _combine_kernel(p0_ref, p1_ref, x_ref, g_ref, ga_ref, gb_ref, y_hbm, o_ref, ya_ref, yb_ref, sem):
    i = pl.program_id(0)
    rows = ya_ref.shape[1]
    slot = lax.rem(i, 2)

    def gather(tile, s, wait):
        _row_gather(p0_ref, tile * rows, y_hbm, ya_ref.at[s], sem.at[0, s], wait)
        _row_gather(p1_ref, tile * rows, y_hbm, yb_ref.at[s], sem.at[1, s], wait)

    @pl.when(i == 0)
    def _():
        gather(0, 0, False)

    @pl.when(i + 1 < pl.num_programs(0))
    def _():
        gather(i + 1, 1 - slot, False)

    gather(i, slot, True)
    o_ref[...] = x_ref[...] + g_ref[0] * (ya_ref[slot] * ga_ref[...] + yb_ref[slot] * gb_ref[...])


def moe_combine(x, gate, y, pos, route_gates, n_ctx, t_lat, tm=256):
    m, d = x.shape
    tm = _pick(math.gcd(n_ctx, t_lat), tm)
    pos0, pos1 = pos[:, 0], pos[:, 1]
    ga, gb = route_gates[:, 0:1], route_gates[:, 1:2]
    return pl.pallas_call(
        _combine_kernel,
        grid_spec=pltpu.PrefetchScalarGridSpec(
            num_scalar_prefetch=2,
            grid=(m // tm,),
            in_specs=[pl.BlockSpec((tm, d), lambda i, a, b: (i, 0)),
                      pl.BlockSpec((1, 1, d), lambda i, a, b: (_group_of_tile(i, tm, n_ctx, t_lat), 0, 0)),
                      pl.BlockSpec((tm, 1), lambda i, a, b: (i, 0)),
                      pl.BlockSpec((tm, 1), lambda i, a, b: (i, 0)),
                      pl.BlockSpec(memory_space=pl.ANY)],
            out_specs=pl.BlockSpec((tm, d), lambda i, a, b: (i, 0)),
            scratch_shapes=[pltpu.VMEM((2, tm, d), F32), pltpu.VMEM((2, tm, d), F32),
                            pltpu.SemaphoreType.DMA((2, 2))]),
        out_shape=jax.ShapeDtypeStruct((m, d), F32),
        compiler_params=_cparams("arbitrary"),
        name="moe_combine",
    )(pos0, pos1, x, gate, ga, gb, y)


def moe_layer(x, gain, shift, scale, gate, router, w_gate, w_up, w_down, n_ctx, t_lat):
    m, _ = x.shape
    n_exp = router.shape[1]
    h, idx, gates = moe_route(x, gain, shift, scale, router, n_ctx, t_lat)
    expert = idx.T.reshape(-1)
    n_assign = expert.shape[0]
    onehot = (expert[:, None] == jnp.arange(n_exp)[None, :]).astype(jnp.int32)
    csum = jnp.cumsum(onehot, axis=0)
    rank = jnp.sum(onehot * (csum - onehot), axis=1)
    counts = csum[-1]
    padded = (counts + MOE_TILE - 1) // MOE_TILE * MOE_TILE
    pad_end = jnp.cumsum(padded)
    dest = jnp.sum(onehot * (pad_end - padded)[None, :], axis=1) + rank
    n_tiles = n_assign // MOE_TILE + n_exp
    token = jnp.arange(n_assign, dtype=jnp.int32) // TOP_K
    src = jnp.zeros((n_tiles * MOE_TILE,), jnp.int32).at[dest].set(token)
    tile_first = ((pad_end - padded) // MOE_TILE).astype(jnp.int32)
    tile_count = (padded // MOE_TILE).astype(jnp.int32)
    pos = dest.reshape(m, TOP_K).astype(jnp.int32)

    xs = gather_rows(h, src, n_tiles)
    u = grouped_matmul(xs, (w_gate, w_up), tile_first, tile_count, BF16, tn=1024, chunk_tiles=2, swiglu=True)
    y = grouped_matmul(u, (w_down,), tile_first, tile_count, F32, tn=512, chunk_tiles=1)
    return moe_combine(x, gate, y, pos, gates.T, n_ctx, t_lat)


def _final_norm_kernel(x_ref, g_ref, o_ref):
    o_ref[...] = _rms(x_ref[...]) * g_ref[...]


def final_rmsnorm(x, gain, row0, n_rows, tm=512):
    d = x.shape[1]
    tm = _pick(math.gcd(row0, n_rows) if row0 else n_rows, tm)
    blk0 = row0 // tm
    return pl.pallas_call(
        _final_norm_kernel,
        grid=(n_rows // tm,),
        in_specs=[pl.BlockSpec((tm, d), lambda i: (blk0 + i, 0)), pl.BlockSpec((1, d), lambda i: (0, 0))],
        out_specs=pl.BlockSpec((tm, d), lambda i: (i, 0)),
        out_shape=jax.ShapeDtypeStruct((n_rows, d), F32),
        compiler_params=_cparams("parallel"),
        name="final_norm",
    )(x, gain.reshape(1, d))


def kernel(x_prompt, x_sample, c, cache_k_l0, cache_v_l0, cache_k_l1, cache_v_l1, cache_k_l2, cache_v_l2, cache_k_l3, cache_v_l3, c_ctx, final_norm, norm1_l0, norm2_l0, ada_w_l0, ada_b_l0, attn_qkv_l0, attn_out_l0, lambda_q1_l0, lambda_k1_l0, lambda_q2_l0, lambda_k2_l0, subln_l0, ffn_gate_l0, ffn_up_l0, ffn_down_l0, norm1_l1, norm2_l1, ada_w_l1, ada_b_l1, attn_qkv_l1, attn_out_l1, q_norm_l1, k_norm_l1, router_l1, moe_gate_l1, moe_up_l1, moe_down_l1, norm1_l2, norm2_l2, ada_w_l2, ada_b_l2, attn_qkv_l2, attn_out_l2, rpb_l2, ffn_gate_l2, ffn_up_l2, ffn_down_l2, norm1_l3, norm2_l3, ada_w_l3, ada_b_l3, attn_qkv_l3, attn_out_l3, lambda_q1_l3, lambda_k1_l3, lambda_q2_l3, lambda_k2_l3, subln_l3, router_l3, moe_gate_l3, moe_up_l3, moe_down_l3):
    final_gain = final_norm
    mixers = [
        ("diff", attn_qkv_l0, attn_out_l0, (lambda_q1_l0, lambda_k1_l0, lambda_q2_l0, lambda_k2_l0, subln_l0)),
        ("gqa", attn_qkv_l1, attn_out_l1, (q_norm_l1, k_norm_l1)),
        ("na", attn_qkv_l2, attn_out_l2, (rpb_l2,)),
        ("diff", attn_qkv_l3, attn_out_l3, (lambda_q1_l3, lambda_k1_l3, lambda_q2_l3, lambda_k2_l3, subln_l3)),
    ]
    ffns = [
        (ffn_gate_l0, ffn_up_l0, ffn_down_l0),
        (router_l1, moe_gate_l1, moe_up_l1, moe_down_l1),
        (ffn_gate_l2, ffn_up_l2, ffn_down_l2),
        (router_l3, moe_gate_l3, moe_up_l3, moe_down_l3),
    ]
    caches = [(cache_k_l0, cache_v_l0), (cache_k_l1, cache_v_l1),
              (cache_k_l2, cache_v_l2), (cache_k_l3, cache_v_l3)]
    norms = [(norm1_l0, norm2_l0), (norm1_l1, norm2_l1), (norm1_l2, norm2_l2), (norm1_l3, norm2_l3)]
    adas = [(ada_w_l0, ada_b_l0), (ada_w_l1, ada_b_l1), (ada_w_l2, ada_b_l2), (ada_w_l3, ada_b_l3)]

    n_batch, seq, d = x_prompt.shape
    bs, t_lat, _ = x_sample.shape
    n_ctx = n_batch * seq
    m = n_ctx + bs * t_lat
    n_heads = d // HEAD_DIM
    n_groups = 1 + bs
    assert n_groups <= 8
    rows_of = dict(n_ctx=n_ctx, t_lat=t_lat)

    x = jnp.concatenate([x_prompt.reshape(n_ctx, d), x_sample.reshape(bs * t_lat, d)], axis=0)
    cond8 = jnp.zeros((8, d), F32).at[0].set(c_ctx).at[1:n_groups].set(c)

    new_state = []
    for i, (mode, w_qkv, w_out, params) in enumerate(mixers):
        gain1, gain2 = norms[i]
        mod = adaln(cond8, *adas[i])[:n_groups].reshape(n_groups, 6, 1, d)
        sh1, sc1, g1, sh2, sc2, g2 = (mod[:, k] for k in range(6))
        n_kv = caches[i][0].shape[1]
        lam_init = 0.8 - 0.6 * math.exp(-0.3 * i)

        h = modulate_rows(x, gain1, sh1, sc1, n_ctx, t_lat)
        qkv = rw_matmul(h, (w_qkv,), F32, tm=1024, tn=1024, **rows_of)
        ctx_mode = "mha" if mode == "na" else mode
        ctx_params = () if mode == "na" else params
        o, new_k, new_v = ctx_attention(qkv, ctx_params, ctx_mode, lam_init, n_batch, seq, n_heads, n_kv, m)
        if mode == "na":
            o = na_latent_attention(o, qkv, *caches[i], params[0], n_ctx, bs, t_lat, n_heads)
        else:
            o = latent_attention(o, qkv, *caches[i], params, mode, lam_init, n_ctx, bs, t_lat, n_heads, n_kv)
        new_state += [new_k, new_v]
        x = rw_matmul(o, (w_out,), F32, tm=1024, tn=1024, epilogue="residual", res=x, gate=g1, **rows_of)

        if i % 2 == 0:
            w_gate, w_up, w_down = ffns[i]
            h = modulate_rows(x, gain2, sh2, sc2, n_ctx, t_lat)
            u = rw_matmul(h, (w_gate, w_up), BF16, tm=1024, tn=512, epilogue="swiglu", **rows_of)
            x = rw_matmul(u, (w_down,), F32, tm=512, tn=512, epilogue="residual", res=x, gate=g2,
                          w_buffers=1, **rows_of)
        else:
            x = moe_layer(x, gain2, sh2, sc2, g2, *ffns[i], n_ctx, t_lat)

    y_ctx = final_rmsnorm(x, final_gain, 0, n_ctx)
    y_lat = final_rmsnorm(x, final_gain, n_ctx, bs * t_lat)
    return (y_ctx.reshape(n_batch, seq, d), y_lat.reshape(bs, t_lat, d), *new_state)
```

```python
import functools
import math

import numpy as np
import jax
import jax.numpy as jnp
from jax import lax
from jax.experimental import pallas as pl
from jax.experimental.pallas import tpu as pltpu

F32 = jnp.float32
BF16 = jnp.bfloat16

HEAD_DIM = 128
GRID_W = 64
NA_ROWS = 8
NA_COLS = 16
NA_QROWS = 4
NA_BAND = 12
TOP_K = 2
ROPE_THETA = 10000.0
NORM_EPS = 1e-6
MASK_VALUE = -1e30
LOG2E = math.log2(math.e)
VMEM_LIMIT_BYTES = 56 * 2**20
MOE_TILE = 256
CAST_ROWS = 256
KEY_CHUNK = 512
CTX_HEADS_PER_STEP = 4
ROW_DMA_UNROLL = 8

def _cparams(*sem):
    return pltpu.CompilerParams(dimension_semantics=sem, vmem_limit_bytes=VMEM_LIMIT_BYTES)


def _pick(n, pref, mult=128):
    t = min(pref, n)
    t -= t % mult
    while t > mult and n % t:
        t -= mult
    assert t >= mult and n % t == 0, (n, pref)
    return t


def _dot(a, b):
    return jnp.dot(a, b, preferred_element_type=F32)


def _dot_nt(a, b):
    return lax.dot_general(a, b, (((1,), (1,)), ((), ())), preferred_element_type=F32)


def _rms(x):
    return x * lax.rsqrt(jnp.mean(x * x, axis=-1, keepdims=True) + NORM_EPS)


def _modulate(x, gain, shift, scale):
    return _rms(x) * gain * (1.0 + scale) + shift


def _group_of_tile(i, tm, n_ctx, t_lat):
    row = i * tm
    return jnp.where(row < n_ctx, 0, 1 + (row - n_ctx) // t_lat)


def _stage_bf16(src_ref, dst_ref):
    rows = math.gcd(src_ref.shape[0], CAST_ROWS)

    def body(i, c):
        r = pl.multiple_of(i * rows, rows)
        dst_ref[pl.ds(r, rows), :] = src_ref[pl.ds(r, rows), :].astype(BF16)
        return c

    lax.fori_loop(0, src_ref.shape[0] // rows, body, 0)


def _adaln_kernel(c_ref, w_ref, b_ref, o_ref):
    c = c_ref[...]
    s = c * jax.nn.sigmoid(c)
    o_ref[...] = _dot(s.astype(BF16), w_ref[...].astype(BF16)) + b_ref[...]


def adaln(cond8, w, b):
    d, n = w.shape
    tn = _pick(n, 1024)
    return pl.pallas_call(
        _adaln_kernel,
        grid=(n // tn,),
        in_specs=[pl.BlockSpec((8, d), lambda j: (0, 0)),
                  pl.BlockSpec((d, tn), lambda j: (0, j)),
                  pl.BlockSpec((1, tn), lambda j: (0, j))],
        out_specs=pl.BlockSpec((8, tn), lambda j: (0, j)),
        out_shape=jax.ShapeDtypeStruct((8, n), F32),
        compiler_params=_cparams("parallel"),
        name="adaln",
    )(cond8, w, b.reshape(1, n))


def _modulate_kernel(x_ref, g_ref, sh_ref, sc_ref, o_ref):
    o_ref[...] = _modulate(x_ref[...], g_ref[...], sh_ref[0], sc_ref[0]).astype(o_ref.dtype)


def modulate_rows(x, gain, shift, scale, n_ctx, t_lat, tm=512):
    m, d = x.shape
    tm = _pick(math.gcd(n_ctx, t_lat), tm)
    grp = lambda i: (_group_of_tile(i, tm, n_ctx, t_lat), 0, 0)
    return pl.pallas_call(
        _modulate_kernel,
        grid=(m // tm,),
        in_specs=[pl.BlockSpec((tm, d), lambda i: (i, 0)),
                  pl.BlockSpec((1, d), lambda i: (0, 0)),
                  pl.BlockSpec((1, 1, d), grp),
                  pl.BlockSpec((1, 1, d), grp)],
        out_specs=pl.BlockSpec((tm, d), lambda i: (i, 0)),
        out_shape=jax.ShapeDtypeStruct((m, d), BF16),
        compiler_params=_cparams("parallel"),
        name="modulate",
    )(x, gain.reshape(1, d), shift, scale)


def _rw_matmul_kernel(*refs, n_w, epilogue):
    a_ref = refs[0]
    w_refs = refs[1:1 + n_w]
    extra = refs[1 + n_w:-1 - n_w]
    o_ref = refs[-1 - n_w]
    wb_refs = refs[-n_w:]

    @pl.when(pl.program_id(1) == 0)
    def _():
        for w_ref, wb_ref in zip(w_refs, wb_refs):
            _stage_bf16(w_ref, wb_ref)

    a = a_ref[...]
    if epilogue == "swiglu":
        g = _dot(a, wb_refs[0][...])
        u = _dot(a, wb_refs[1][...])
        o = g * jax.nn.sigmoid(g) * u
    elif epilogue == "residual":
        res_ref, gate_ref = extra
        o = res_ref[...] + gate_ref[0] * _dot(a, wb_refs[0][...])
    else:
        o = _dot(a, wb_refs[0][...])
    o_ref[...] = o.astype(o_ref.dtype)


def rw_matmul(a, ws, out_dtype, tm, tn, epilogue="none", res=None, gate=None, n_ctx=None, t_lat=None,
              w_buffers=2):
    m, kdim = a.shape
    n = ws[0].shape[1]
    tm = _pick(math.gcd(n_ctx, t_lat) if n_ctx else m, tm)
    tn = _pick(n, tn)
    in_specs = [pl.BlockSpec((tm, kdim), lambda j, i: (i, 0))]
    in_specs += [pl.BlockSpec((kdim, tn), lambda j, i: (0, j), pipeline_mode=pl.Buffered(w_buffers)) for _ in ws]
    args = [a, *ws]
    if epilogue == "residual":
        in_specs += [pl.BlockSpec((tm, tn), lambda j, i: (i, j)),
                     pl.BlockSpec((1, 1, tn), lambda j, i: (_group_of_tile(i, tm, n_ctx, t_lat), 0, j))]
        args += [res, gate]
    return pl.pallas_call(
        functools.partial(_rw_matmul_kernel, n_w=len(ws), epilogue=epilogue),
        grid=(n // tn, m // tm),
        in_specs=in_specs,
        out_specs=pl.BlockSpec((tm, tn), lambda j, i: (i, j)),
        out_shape=jax.ShapeDtypeStruct((m, n), out_dtype),
        scratch_shapes=[pltpu.VMEM((kdim, tn), BF16) for _ in ws],
        compiler_params=_cparams("parallel", "arbitrary"),
        name="matmul_" + epilogue,
    )(*args)


def _softmax_rows(s):
    p = jnp.exp(s - jnp.max(s, axis=-1, keepdims=True))
    return p * (1.0 / jnp.sum(p, axis=-1, keepdims=True))


def _diff_lambda(lq1_ref, lk1_ref, lq2_ref, lk2_ref, lam_init):
    a = jnp.sum(lq1_ref[...] * lk1_ref[...], axis=-1, keepdims=True)
    b = jnp.sum(lq2_ref[...] * lk2_ref[...], axis=-1, keepdims=True)
    return jnp.exp(a) - jnp.exp(b) + lam_init


def _split_maps(q, scale):
    first = lax.broadcasted_iota(jnp.int32, (1, HEAD_DIM), 1) < HEAD_DIM // 2
    q = q * scale
    return jnp.where(first, q, 0.0).astype(BF16), jnp.where(first, 0.0, q).astype(BF16)


def _diff_attend(q, k_bf, v_bf, lam, subln, lam_init):
    q1, q2 = _split_maps(q, 1.0)
    scale = (HEAD_DIM // 2) ** -0.5
    p1 = _softmax_rows(_dot_nt(q1, k_bf) * scale)
    p2 = _softmax_rows(_dot_nt(q2, k_bf) * scale)
    o = _dot((p1 - lam * p2).astype(BF16), v_bf)
    return _rms(o) * subln * (1.0 - lam_init)


def _plain_attend(q_bf, k_bf, v_bf):
    s = _dot_nt(q_bf, k_bf) * (HEAD_DIM ** -0.5)
    p = jnp.exp(s - jnp.max(s, axis=-1, keepdims=True))
    return _dot(p.astype(BF16), v_bf) * (1.0 / jnp.sum(p, axis=-1, keepdims=True))


def _online_softmax_pv(q_bf, k_ref, vaug_ref, chunks):
    nq = q_bf.shape[0]
    m = jnp.full((nq, 1), -jnp.inf, F32)
    acc = jnp.zeros((nq, 2 * HEAD_DIM), F32)
    for start, size in chunks:
        s = _dot_nt(q_bf, k_ref[start:start + size, :])
        m_new = jnp.maximum(m, jnp.max(s, axis=-1, keepdims=True))
        p = jnp.exp2(s - m_new)
        acc = acc * jnp.exp2(m - m_new) +_dot(p.astype(BF16), vaug_ref[start:start + size, :])
        m = m_new
    return acc


def _rope(x, cos, sin, quarter):
    lane = lax.broadcasted_iota(jnp.int32, (1, HEAD_DIM), 1)
    even = (lane // quarter) % 2 == 0
    nxt = pltpu.roll(x, HEAD_DIM - quarter, 1)
    prv = pltpu.roll(x, quarter, 1)
    return x * cos + jnp.where(even, -nxt, prv) * sin


def _rope_tables(t_lat, dim):
    quarter = dim // 4
    inv_freq = ROPE_THETA ** (-jnp.arange(quarter, dtype=F32) / quarter)
    t = jnp.arange(t_lat)
    ang_r = (t // GRID_W).astype(F32)[:, None] * inv_freq
    ang_c = (t % GRID_W).astype(F32)[:, None] * inv_freq
    ang = jnp.concatenate([ang_r, ang_r, ang_c, ang_c] * (HEAD_DIM // dim), axis=-1)
    return jnp.cos(ang), jnp.sin(ang)


def _ctx_attn_kernel(*refs, mode, lam_init, n_q, n_kv):
    q_ref, k_ref, v_ref = refs[:3]
    o_ref, nk_ref, nv_ref = refs[-3:]
    heads = lambda ref, i: ref[:, i * HEAD_DIM:(i + 1) * HEAD_DIM]
    ks, vs = [], []
    for j in range(n_kv):
        k, v = heads(k_ref, j), heads(v_ref, j)
        if mode == "gqa":
            k = _rms(k) * refs[4][...]
        nk_ref[j] = k
        nv_ref[j] = v
        ks.append(k.astype(BF16))
        vs.append(v.astype(BF16))
    if mode == "diff":
        lam = _diff_lambda(*refs[3:7], lam_init)
    for i in range(n_q):
        q = heads(q_ref, i)
        k_bf, v_bf = ks[i % n_kv], vs[i % n_kv]
        if mode == "gqa":
            q = _rms(q) * refs[3][...]
        if mode == "diff":
            o = _diff_attend(q, k_bf, v_bf, lam, refs[7][...], lam_init)
        else:
            o = _plain_attend(q.astype(BF16), k_bf, v_bf)
        o_ref[:, i * HEAD_DIM:(i + 1) * HEAD_DIM] = o.astype(o_ref.dtype)


def ctx_attention(qkv, params, mode, lam_init, n_batch, seq, n_heads, n_kv, m_total):
    grp = n_heads // n_kv
    n_q = grp if grp > 1 else math.gcd(CTX_HEADS_PER_STEP, n_heads)
    n_k = 1 if grp > 1 else n_q
    qw, kw = n_q * HEAD_DIM, n_k * HEAD_DIM
    k_off = n_heads * HEAD_DIM // kw
    v_off = (n_heads + n_kv) * HEAD_DIM // kw
    small = [pl.BlockSpec((1, p.shape[-1]), lambda b, g: (0, 0)) for p in params]
    state = pl.BlockSpec((None, n_k, seq, HEAD_DIM), lambda b, g: (b, g, 0, 0))
    st_shape = jax.ShapeDtypeStruct((n_batch, n_kv, seq, HEAD_DIM), F32)
    return pl.pallas_call(
        functools.partial(_ctx_attn_kernel, mode=mode, lam_init=lam_init, n_q=n_q, n_kv=n_k),
        grid=(n_batch, n_heads // n_q),
        in_specs=[pl.BlockSpec((seq, qw), lambda b, g: (b, g)),
                  pl.BlockSpec((seq, kw), lambda b, g: (b, k_off + g)),
                  pl.BlockSpec((seq, kw), lambda b, g: (b, v_off + g))] + small,
        out_specs=[pl.BlockSpec((seq, qw), lambda b, g: (b, g)), state, state],
        out_shape=[jax.ShapeDtypeStruct((m_total, n_heads * HEAD_DIM), BF16), st_shape, st_shape],
        compiler_params=_cparams("parallel", "parallel"),
        name="ctx_attention_" + mode,
    )(qkv, qkv, qkv, *[p.reshape(1, -1) for p in params])


def _lat_attn_kernel(*refs, mode, lam_init, n_ctx_keys, grp, quarter, chunks):
    (o_in_ref, q_ref, kl_ref, vl_ref, ck_ref, cv_ref,
     cosq_ref, sinq_ref, cosk_ref, sink_ref) = refs[:10]
    o_ref, kall_ref, vaug_ref = refs[-3:]
    del o_in_ref
    params = refs[10:-3]
    h, qi = pl.program_id(1), pl.program_id(2)

    @pl.when(jnp.logical_and(qi == 0, h % grp == 0))
    def _():
        k = kl_ref[...]
        if mode == "gqa":
            k = _rms(k) * params[1][...]
        kall_ref[:n_ctx_keys, :] = ck_ref[...].astype(BF16)
        kall_ref[n_ctx_keys:, :] = _rope(k, cosk_ref[...], sink_ref[...], quarter).astype(BF16)
        vaug_ref[:n_ctx_keys, :HEAD_DIM] = cv_ref[...].astype(BF16)
        vaug_ref[n_ctx_keys:, :HEAD_DIM] = vl_ref[...].astype(BF16)
        vaug_ref[:, HEAD_DIM:] = jnp.ones((vaug_ref.shape[0], HEAD_DIM), BF16)

    q = q_ref[...]
    if mode == "gqa":
        q = _rms(q) * params[0][...]
    q = _rope(q, cosq_ref[...], sinq_ref[...], quarter)
    if mode == "diff":
        lam = _diff_lambda(*params[:4], lam_init)
        q1, q2 = _split_maps(q, (HEAD_DIM // 2) ** -0.5 * LOG2E)
        a1 = _online_softmax_pv(q1, kall_ref, vaug_ref, chunks)
        a2 = _online_softmax_pv(q2, kall_ref, vaug_ref, chunks)
        o = a1[:, :HEAD_DIM] * (1.0 / a1[:, HEAD_DIM:]) - a2[:, :HEAD_DIM] * (lam / a2[:, HEAD_DIM:])
        o = _rms(o) * params[4][...] * (1.0 - lam_init)
    else:
        a = _online_softmax_pv((q * (HEAD_DIM ** -0.5 * LOG2E)).astype(BF16), kall_ref, vaug_ref, chunks)
        o = a[:, :HEAD_DIM] * (1.0 / a[:, HEAD_DIM:])
    o_ref[...] = o.astype(o_ref.dtype)


def latent_attention(o_ctx, qkv, cache_k, cache_v, params, mode, lam_init,
                     n_ctx, n_batch, t_lat, n_heads, n_kv, tq=1024):
    grp = n_heads // n_kv
    n_past = cache_k.shape[2]
    tq = _pick(math.gcd(n_ctx, t_lat), tq)
    nq = t_lat // tq
    assert n_ctx % t_lat == 0
    base_q, base_k = n_ctx // tq, n_ctx // t_lat
    dim = HEAD_DIM // 2 if mode == "diff" else HEAD_DIM
    cos, sin = _rope_tables(t_lat, dim)
    kc = _pick(t_lat, KEY_CHUNK)
    chunks = ((0, n_past),) + tuple((n_past + i * kc, kc) for i in range(t_lat // kc))
    qblk = pl.BlockSpec((tq, HEAD_DIM), lambda b, h, i: (base_q + b * nq + i, h))
    kvblk = lambda off: pl.BlockSpec((t_lat, HEAD_DIM), lambda b, h, i: (base_k + b, off + h // grp))
    cblk = pl.BlockSpec((None, None, n_past, HEAD_DIM), lambda b, h, i: (b, h // grp, 0, 0))
    tabq = pl.BlockSpec((tq, HEAD_DIM), lambda b, h, i: (i, 0))
    tabk = pl.BlockSpec((t_lat, HEAD_DIM), lambda b, h, i: (0, 0))
    small = [pl.BlockSpec((1, p.shape[-1]), lambda b, h, i: (0, 0)) for p in params]
    return pl.pallas_call(
        functools.partial(_lat_attn_kernel, mode=mode, lam_init=lam_init,
                          n_ctx_keys=n_past, grp=grp, quarter=dim // 4, chunks=chunks),
        grid=(n_batch, n_heads, nq),
        in_specs=[pl.BlockSpec(memory_space=pl.ANY), qblk, kvblk(n_heads), kvblk(n_heads + n_kv),
                  cblk, cblk, tabq, tabq, tabk, tabk] + small,
        out_specs=qblk,
        out_shape=jax.ShapeDtypeStruct(o_ctx.shape, o_ctx.dtype),
        scratch_shapes=[pltpu.VMEM((n_past + t_lat, HEAD_DIM), BF16),
                        pltpu.VMEM((n_past + t_lat, 2 * HEAD_DIM), BF16)],
        input_output_aliases={0: 0},
        compiler_params=_cparams("parallel", "arbitrary", "arbitrary"),
        name="latent_attention_" + mode,
    )(o_ctx, qkv, qkv, qkv, cache_k, cache_v, cos, sin, cos, sin,
      *[p.reshape(1, -1) for p in params])


def _na_geometry(rows):
    kh = min(NA_ROWS, rows)
    band = min(NA_BAND, rows)
    nblk = rows // NA_QROWS
    qr = np.arange(NA_QROWS)[:, None]
    kr = np.arange(band)[None, :]
    qc = np.arange(GRID_W)[:, None]
    kc = np.arange(GRID_W)[None, :]
    cs = np.clip(qc - NA_COLS // 2, 0, GRID_W - NA_COLS)
    col_ok = (kc >= cs) & (kc < cs + NA_COLS)
    col_idx = np.where(col_ok, kc - qc + NA_COLS - 1, 0)
    starts, pat_ids, pats, keys = [], [], [], []
    for j in range(nblk):
        r0 = j * NA_QROWS
        bs = int(np.clip(r0 - kh // 2, 0, rows - band))
        r = r0 + qr
        rs = np.clip(r - kh // 2, 0, rows - kh)
        key_row = bs + kr
        row_ok = (key_row >= rs) & (key_row < rs + kh)
        row_idx = np.where(row_ok, key_row - r + NA_ROWS - 1, 0)
        key = (bs - r0, tuple((rs - r0).ravel()))
        if key not in keys:
            keys.append(key)
            pats.append((row_ok[:, None, :, None] & col_ok[None, :, None, :], row_idx))
        starts.append(bs)
        pat_ids.append(keys.index(key))
    return starts, pat_ids, pats, col_idx


def _na_bias_tables(rpb, rows):
    starts, pat_ids, pats, col_idx = _na_geometry(rows)
    n_heads = rpb.shape[0]
    col_onehot = jnp.asarray(col_idx[:, :, None] == np.arange(rpb.shape[2]), F32)
    tables = []
    for valid, row_idx in pats:
        by_row = rpb[:, row_idx, :]
        full = jnp.einsum("hqkj,cdj->hqckd", by_row, col_onehot, precision=lax.Precision.HIGHEST)
        full = jnp.where(valid[None], full, MASK_VALUE)
        tables.append(full.reshape(n_heads, valid.shape[0] * GRID_W, valid.shape[2] * GRID_W))
    return starts, pat_ids, jnp.stack(tables)


def _na_attn_kernel(start_ref, pat_ref, o_in_ref, q_ref, kl_ref, vl_ref, ck_ref, cv_ref, bias_ref,
                    o_ref, *, band_keys):
    del pat_ref, o_in_ref
    start = pl.multiple_of(start_ref[pl.program_id(2)] * GRID_W, GRID_W)
    scale = HEAD_DIM ** -0.5
    q = q_ref[...].astype(BF16)
    kb = kl_ref[pl.ds(start, band_keys), :].astype(BF16)
    vb = vl_ref[pl.ds(start, band_keys), :].astype(BF16)
    s_nb = _dot_nt(q, kb) * scale + bias_ref[...]
    s_cx = _dot_nt(q, ck_ref[...].astype(BF16)) * scale
    m = jnp.maximum(jnp.max(s_nb, axis=-1, keepdims=True), jnp.max(s_cx, axis=-1, keepdims=True))
    p_nb = jnp.exp(s_nb - m)
    p_cx = jnp.exp(s_cx - m)
    l = jnp.sum(p_nb, axis=-1, keepdims=True) + jnp.sum(p_cx, axis=-1, keepdims=True)
    o = _dot(p_nb.astype(BF16), vb) + _dot(p_cx.astype(BF16), cv_ref[...].astype(BF16))
    o_ref[...] = (o * (1.0 / l)).astype(o_ref.dtype)


def na_latent_attention(o_ctx, qkv, cache_k, cache_v, rpb, n_ctx, n_batch, t_lat, n_heads):
    rows = t_lat // GRID_W
    assert rows % NA_QROWS == 0 and n_ctx % t_lat == 0
    starts, pat_ids, bias = _na_bias_tables(rpb, rows)
    nblk = len(starts)
    tq = NA_QROWS * GRID_W
    band_keys = min(NA_BAND, rows) * GRID_W
    n_past = cache_k.shape[2]
    base_q, base_k = n_ctx // tq, n_ctx // t_lat
    qblk = pl.BlockSpec((tq, HEAD_DIM), lambda b, h, j, st, pt: (base_q + b * nblk + j, h))
    kvblk = lambda off: pl.BlockSpec((t_lat, HEAD_DIM), lambda b, h, j, st, pt: (base_k + b, off + h))
    cblk = pl.BlockSpec((None, None, n_past, HEAD_DIM), lambda b, h, j, st, pt: (b, h, 0, 0))
    bblk = pl.BlockSpec((None, None, tq, band_keys), lambda b, h, j, st, pt: (pt[j], h, 0, 0))
    return pl.pallas_call(
        functools.partial(_na_attn_kernel, band_keys=band_keys),
        grid_spec=pltpu.PrefetchScalarGridSpec(
            num_scalar_prefetch=2,
            grid=(n_batch, n_heads, nblk),
            in_specs=[pl.BlockSpec(memory_space=pl.ANY), qblk, kvblk(n_heads), kvblk(2 * n_heads),
                      cblk, cblk, bblk],
            out_specs=qblk),
        out_shape=jax.ShapeDtypeStruct(o_ctx.shape, o_ctx.dtype),
        input_output_aliases={2: 0},
        compiler_params=_cparams("parallel", "arbitrary", "arbitrary"),
        name="latent_attention_na",
    )(jnp.asarray(starts, jnp.int32), jnp.asarray(pat_ids, jnp.int32),
      o_ctx, qkv, qkv, qkv, cache_k, cache_v, bias)


def _router_kernel(x_ref, g_ref, sh_ref, sc_ref, rt_ref, h_ref, idx_ref, gate_ref):
    h = _modulate(x_ref[...], g_ref[...], sh_ref[0], sc_ref[0])
    h_ref[...] = h
    logits = lax.dot_general(rt_ref[...], h, (((1,), (1,)), ((), ())),
                             precision=lax.Precision.HIGHEST, preferred_element_type=F32)
    n_exp = logits.shape[0]
    eid = lax.broadcasted_iota(jnp.int32, logits.shape, 0)
    m1 = jnp.max(logits, axis=0, keepdims=True)
    i1 = jnp.min(jnp.where(logits == m1, eid, n_exp), axis=0, keepdims=True)
    rest = jnp.where(eid == i1, -jnp.inf, logits)
    m2 = jnp.max(rest, axis=0, keepdims=True)
    i2 = jnp.min(jnp.where(rest == m2, eid, n_exp), axis=0, keepdims=True)
    e = jnp.exp(m2 - m1)
    idx_ref[0:1, :] = i1
    idx_ref[1:2, :] = i2
    gate_ref[0:1, :] = 1.0 / (1.0 + e)
    gate_ref[1:2, :] = e / (1.0 + e)


def moe_route(x, gain, shift, scale, router, n_ctx, t_lat, tm=512):
    m, d = x.shape
    n_exp = router.shape[1]
    tm = _pick(math.gcd(n_ctx, t_lat), tm)
    grp = lambda i: (_group_of_tile(i, tm, n_ctx, t_lat), 0, 0)
    return pl.pallas_call(
        _router_kernel,
        grid=(m // tm,),
        in_specs=[pl.BlockSpec((tm, d), lambda i: (i, 0)),
                  pl.BlockSpec((1, d), lambda i: (0, 0)),
                  pl.BlockSpec((1, 1, d), grp),
                  pl.BlockSpec((1, 1, d), grp),
                  pl.BlockSpec((n_exp, d), lambda i: (0, 0))],
        out_specs=[pl.BlockSpec((tm, d), lambda i: (i, 0)),
                   pl.BlockSpec((TOP_K, tm), lambda i: (0, i)),
                   pl.BlockSpec((TOP_K, tm), lambda i: (0, i))],
        out_shape=[jax.ShapeDtypeStruct((m, d), F32),
                   jax.ShapeDtypeStruct((TOP_K, m), jnp.int32),
                   jax.ShapeDtypeStruct((TOP_K, m), F32)],
        compiler_params=_cparams("parallel"),
        name="moe_router",
    )(x, gain.reshape(1, d), shift, scale, router.T)


def _row_gather(idx_ref, base, src_hbm, dst_ref, sem, wait):
    def body(r, c):
        cp = pltpu.make_async_copy(src_hbm.at[pl.ds(idx_ref[base + r], 1)], dst_ref.at[pl.ds(r, 1)], sem)
        if wait:
            cp.wait()
        else:
            cp.start()
        return c

    lax.fori_loop(0, dst_ref.shape[0], body, 0, unroll=ROW_DMA_UNROLL)


def _gather_rows_kernel(src_ref, h_hbm, o_ref, buf_ref, sem):
    t = pl.program_id(0)
    rows = buf_ref.shape[1]
    slot = lax.rem(t, 2)

    @pl.when(t == 0)
    def _():
        _row_gather(src_ref, 0, h_hbm, buf_ref.at[0], sem.at[0], wait=False)

    @pl.when(t + 1 < pl.num_programs(0))
    def _():
        _row_gather(src_ref, (t + 1) * rows, h_hbm, buf_ref.at[1 - slot], sem.at[1 - slot], wait=False)

    _row_gather(src_ref, t * rows, h_hbm, buf_ref.at[slot], sem.at[slot], wait=True)
    o_ref[...] = buf_ref[slot].astype(o_ref.dtype)


def gather_rows(h, src, n_tiles):
    d = h.shape[1]
    return pl.pallas_call(
        _gather_rows_kernel,
        grid_spec=pltpu.PrefetchScalarGridSpec(
            num_scalar_prefetch=1,
            grid=(n_tiles,),
            in_specs=[pl.BlockSpec(memory_space=pl.ANY)],
            out_specs=pl.BlockSpec((MOE_TILE, d), lambda t, s: (t, 0)),
            scratch_shapes=[pltpu.VMEM((2, MOE_TILE, d), h.dtype), pltpu.SemaphoreType.DMA((2,))]),
        out_shape=jax.ShapeDtypeStruct((n_tiles * MOE_TILE, d), BF16),
        compiler_params=_cparams("arbitrary"),
        name="moe_gather",
    )(src, h)


def _grouped_matmul_kernel(first_ref, count_ref, x_hbm, *refs, n_w, swiglu, n_tiles, tile):
    w_refs = refs[:n_w]
    o_hbm = refs[n_w]
    wb_refs = refs[n_w + 1:2 * n_w + 1]
    xbuf, obuf, xtail, otail, in_sem, out_sem, tail_in_sem, tail_out_sem = refs[2 * n_w + 1:]
    j, e = pl.program_id(0), pl.program_id(1)
    first, count = first_ref[e], count_ref[e]
    tn = obuf.shape[2]
    big = obuf.shape[1] // tile
    n_big = count // big
    n_tail = count - n_big * big
    tail0 = first + n_big * big
    col0 = pl.multiple_of(j * tn, tn)

    def rows(t, n):
        return pl.ds(pl.multiple_of(t * tile, tile), n * tile)

    def in_copy(c, slot):
        return pltpu.make_async_copy(x_hbm.at[rows(first + c * big, big)], xbuf.at[slot], in_sem.at[slot])

    def out_copy(c, slot):
        return pltpu.make_async_copy(obuf.at[slot], o_hbm.at[rows(first + c * big, big), pl.ds(col0, tn)],
                                     out_sem.at[slot])

    def tail_in(s):
        return pltpu.make_async_copy(x_hbm.at[rows(tail0 + s, 1)], xtail.at[s], tail_in_sem.at[s])

    def tail_out(s):
        return pltpu.make_async_copy(otail.at[s], o_hbm.at[rows(tail0 + s, 1), pl.ds(col0, tn)],
                                     tail_out_sem.at[s])

    def apply_weights(x):
        if swiglu:
            a = _dot(x, wb_refs[0][...])
            b = _dot(x, wb_refs[1][...])
            return (a * jax.nn.sigmoid(a) * b).astype(obuf.dtype)
        return _dot(x, wb_refs[0][...]).astype(obuf.dtype)

    @pl.when(count > 0)
    def _():
        @pl.when(n_big > 0)
        def _():
            in_copy(0, 0).start()

        for s in range(big - 1):
            @pl.when(s < n_tail)
            def _():
                tail_in(s).start()

        for w_ref, wb_ref in zip(w_refs, wb_refs):
            _stage_bf16(w_ref, wb_ref)

        def chunk_pair(pair, carry):
            for slot in (0, 1):
                c = 2 * pair + slot

                @pl.when(c < n_big)
                def _():
                    in_copy(c, slot).wait()

                    @pl.when(c + 1 < n_big)
                    def _():
                        in_copy(c + 1, 1 - slot).start()

                    @pl.when(c >= 2)
                    def _():
                        out_copy(c - 2, slot).wait()

                    obuf[slot] = apply_weights(xbuf[slot])
                    out_copy(c, slot).start()
            return carry

        lax.fori_loop(0, (n_big + 1) // 2, chunk_pair, 0)

        for s in range(big - 1):
            @pl.when(s < n_tail)
            def _():
                tail_in(s).wait()
                otail[s] = apply_weights(xtail[s])
                tail_out(s).start()

        for back in (2, 1):
            @pl.when(n_big >= back)
            def _():
                c = n_big - back
                out_copy(c, lax.rem(c, 2)).wait()

        for s in range(big - 1):
            @pl.when(s < n_tail)
            def _():
                tail_out(s).wait()

    @pl.when(e == pl.num_programs(1) - 1)
    def _():
        otail[0] = jnp.zeros((tile, tn), otail.dtype)

        def zero_tile(t, carry):
            cp = pltpu.make_async_copy(otail.at[0], o_hbm.at[rows(t, 1), pl.ds(col0, tn)], tail_out_sem.at[0])
            cp.start()
            cp.wait()
            return carry

        lax.fori_loop(first + count, n_tiles, zero_tile, 0)


def grouped_matmul(xs, ws, tile_first, tile_count, out_dtype, tn, chunk_tiles, swiglu=False, w_buffers=2):
    p, kdim = xs.shape
    n_exp, _, n = ws[0].shape
    tn = _pick(n, tn)
    n_tiles = p // MOE_TILE
    chunk_rows = chunk_tiles * MOE_TILE
    n_tail_bufs = max(chunk_tiles - 1, 1)
    wblk =pl.BlockSpec((None, kdim, tn), lambda j, e, first, count: (e, 0, j),
                        pipeline_mode=pl.Buffered(w_buffers))
    return pl.pallas_call(
        functools.partial(_grouped_matmul_kernel, n_w=len(ws), swiglu=swiglu, n_tiles=n_tiles, tile=MOE_TILE),
        grid_spec=pltpu.PrefetchScalarGridSpec(
            num_scalar_prefetch=2,
            grid=(n // tn, n_exp),
            in_specs=[pl.BlockSpec(memory_space=pl.ANY)] + [wblk for _ in ws],
            out_specs=pl.BlockSpec(memory_space=pl.ANY),
            scratch_shapes=[pltpu.VMEM((kdim, tn), BF16) for _ in ws]
                           + [pltpu.VMEM((2, chunk_rows, kdim), xs.dtype),
                              pltpu.VMEM((2, chunk_rows, tn), out_dtype),
                              pltpu.VMEM((n_tail_bufs, MOE_TILE, kdim), xs.dtype),
                              pltpu.VMEM((n_tail_bufs, MOE_TILE, tn), out_dtype),
                              pltpu.SemaphoreType.DMA((2,)),
                              pltpu.SemaphoreType.DMA((2,)),
                              pltpu.SemaphoreType.DMA((n_tail_bufs,)),
                              pltpu.SemaphoreType.DMA((n_tail_bufs,))]),
        out_shape=jax.ShapeDtypeStruct((p, n), out_dtype),
        compiler_params=_cparams("arbitrary", "arbitrary"),
        name="moe_expert_swiglu" if swiglu else "moe_expert_down",
    )(tile_first, tile_count, xs, *ws)


def _combine_kernel(p0_ref, p1_ref, x_ref, g_ref, ga_ref, gb_ref, y_hbm, o_ref, ya_ref, yb_ref, sem):
    i = pl.program_id(0)
    rows = ya_ref.shape[1]
    slot = lax.rem(i, 2)

    def gather(tile, s, wait):
        _row_gather(p0_ref, tile * rows, y_hbm, ya_ref.at[s], sem.at[0, s], wait)
        _row_gather(p1_ref, tile * rows, y_hbm, yb_ref.at[s], sem.at[1, s], wait)

    @pl.when(i == 0)
    def _():
        gather(0, 0, False)

    @pl.when(i + 1 < pl.num_programs(0))
    def _():
        gather(i + 1, 1 - slot, False)

    gather(i, slot, True)
    o_ref[...] = x_ref[...] + g_ref[0] * (ya_ref[slot] * ga_ref[...] + yb_ref[slot] * gb_ref[...])


def moe_combine(x, gate, y, pos, route_gates, n_ctx, t_lat, tm=256):
    m, d = x.shape
    tm = _pick(math.gcd(n_ctx, t_lat), tm)
    pos0, pos1 = pos[:, 0], pos[:, 1]
    ga, gb = route_gates[:, 0:1], route_gates[:, 1:2]
    return pl.pallas_call(
        _combine_kernel,
        grid_spec=pltpu.PrefetchScalarGridSpec(
            num_scalar_prefetch=2,
            grid=(m // tm,),
            in_specs=[pl.BlockSpec((tm, d), lambda i, a, b: (i, 0)),
                      pl.BlockSpec((1, 1, d), lambda i, a, b: (_group_of_tile(i, tm, n_ctx, t_lat), 0, 0)),
                      pl.BlockSpec((tm, 1), lambda i, a, b: (i, 0)),
                      pl.BlockSpec((tm, 1), lambda i, a, b: (i, 0)),
                      pl.BlockSpec(memory_space=pl.ANY)],
            out_specs=pl.BlockSpec((tm, d), lambda i, a, b: (i, 0)),
            scratch_shapes=[pltpu.VMEM((2, tm, d), F32), pltpu.VMEM((2, tm, d), F32),
                            pltpu.SemaphoreType.DMA((2, 2))]),
        out_shape=jax.ShapeDtypeStruct((m, d), F32),
        compiler_params=_cparams("arbitrary"),
        name="moe_combine",
    )(pos0, pos1, x, gate, ga, gb, y)


def moe_layer(x, gain, shift, scale, gate, router, w_gate, w_up, w_down, n_ctx, t_lat):
    m, _ = x.shape
    n_exp = router.shape[1]
    h, idx, gates = moe_route(x, gain, shift, scale, router, n_ctx, t_lat)
    expert = idx.T.reshape(-1)
    n_assign = expert.shape[0]
    onehot = (expert[:, None] == jnp.arange(n_exp)[None, :]).astype(jnp.int32)
    csum = jnp.cumsum(onehot, axis=0)
    rank = jnp.sum(onehot * (csum - onehot), axis=1)
    counts = csum[-1]
    padded = (counts + MOE_TILE - 1) // MOE_TILE * MOE_TILE
    pad_end = jnp.cumsum(padded)
    dest = jnp.sum(onehot * (pad_end - padded)[None, :], axis=1) + rank
    n_tiles = n_assign // MOE_TILE + n_exp
    token = jnp.arange(n_assign, dtype=jnp.int32) // TOP_K
    src = jnp.zeros((n_tiles * MOE_TILE,), jnp.int32).at[dest].set(token)
    tile_first = ((pad_end - padded) // MOE_TILE).astype(jnp.int32)
    tile_count = (padded // MOE_TILE).astype(jnp.int32)
    pos = dest.reshape(m, TOP_K).astype(jnp.int32)

    xs = gather_rows(h, src, n_tiles)
    u = grouped_matmul(xs, (w_gate, w_up), tile_first, tile_count, BF16, tn=1024, chunk_tiles=2, swiglu=True)
    y = grouped_matmul(u, (w_down,), tile_first, tile_count, F32, tn=512, chunk_tiles=1)
    return moe_combine(x, gate, y, pos, gates.T, n_ctx, t_lat)


def _final_norm_kernel(x_ref, g_ref, o_ref):
    o_ref[...] = _rms(x_ref[...]) * g_ref[...]


def final_rmsnorm(x, gain, row0, n_rows, tm=512):
    d = x.shape[1]
    tm = _pick(math.gcd(row0, n_rows) if row0 else n_rows, tm)
    blk0 = row0 // tm
    return pl.pallas_call(
        _final_norm_kernel,
        grid=(n_rows // tm,),
        in_specs=[pl.BlockSpec((tm, d), lambda i: (blk0 + i, 0)), pl.BlockSpec((1, d), lambda i: (0, 0))],
        out_specs=pl.BlockSpec((tm, d), lambda i: (i, 0)),
        out_shape=jax.ShapeDtypeStruct((n_rows, d), F32),
        compiler_params=_cparams("parallel"),
        name="final_norm",
    )(x, gain.reshape(1, d))


def kernel(x_prompt, x_sample, c, cache_k_l0, cache_v_l0, cache_k_l1, cache_v_l1, cache_k_l2, cache_v_l2, cache_k_l3, cache_v_l3, c_ctx, final_norm, norm1_l0, norm2_l0, ada_w_l0, ada_b_l0, attn_qkv_l0, attn_out_l0, lambda_q1_l0, lambda_k1_l0, lambda_q2_l0, lambda_k2_l0, subln_l0, ffn_gate_l0, ffn_up_l0, ffn_down_l0, norm1_l1, norm2_l1, ada_w_l1, ada_b_l1, attn_qkv_l1, attn_out_l1, q_norm_l1, k_norm_l1, router_l1, moe_gate_l1, moe_up_l1, moe_down_l1, norm1_l2, norm2_l2, ada_w_l2, ada_b_l2, attn_qkv_l2, attn_out_l2, rpb_l2, ffn_gate_l2, ffn_up_l2, ffn_down_l2, norm1_l3, norm2_l3, ada_w_l3, ada_b_l3, attn_qkv_l3, attn_out_l3, lambda_q1_l3, lambda_k1_l3, lambda_q2_l3, lambda_k2_l3, subln_l3, router_l3, moe_gate_l3, moe_up_l3, moe_down_l3):
    final_gain = final_norm
    mixers = [
        ("diff", attn_qkv_l0, attn_out_l0, (lambda_q1_l0, lambda_k1_l0, lambda_q2_l0, lambda_k2_l0, subln_l0)),
        ("gqa", attn_qkv_l1, attn_out_l1, (q_norm_l1, k_norm_l1)),
        ("na", attn_qkv_l2, attn_out_l2, (rpb_l2,)),
        ("diff", attn_qkv_l3, attn_out_l3, (lambda_q1_l3, lambda_k1_l3, lambda_q2_l3, lambda_k2_l3, subln_l3)),
    ]
    ffns = [
        (ffn_gate_l0, ffn_up_l0, ffn_down_l0),
        (router_l1, moe_gate_l1, moe_up_l1, moe_down_l1),
        (ffn_gate_l2, ffn_up_l2, ffn_down_l2),
        (router_l3, moe_gate_l3, moe_up_l3, moe_down_l3),
    ]
    caches = [(cache_k_l0, cache_v_l0), (cache_k_l1, cache_v_l1),
              (cache_k_l2, cache_v_l2), (cache_k_l3, cache_v_l3)]
    norms = [(norm1_l0, norm2_l0), (norm1_l1, norm2_l1), (norm1_l2, norm2_l2), (norm1_l3, norm2_l3)]
    adas = [(ada_w_l0, ada_b_l0), (ada_w_l1, ada_b_l1), (ada_w_l2, ada_b_l2), (ada_w_l3, ada_b_l3)]

    n_batch, seq, d = x_prompt.shape
    bs, t_lat, _ = x_sample.shape
    n_ctx = n_batch * seq
    m = n_ctx + bs * t_lat
    n_heads = d // HEAD_DIM
    n_groups = 1 + bs
    assert n_groups <= 8
    rows_of = dict(n_ctx=n_ctx, t_lat=t_lat)

    x = jnp.concatenate([x_prompt.reshape(n_ctx, d), x_sample.reshape(bs * t_lat, d)], axis=0)
    cond8 = jnp.zeros((8, d), F32).at[0].set(c_ctx).at[1:n_groups].set(c)

    new_state = []
    for i, (mode, w_qkv, w_out, params) in enumerate(mixers):
        gain1, gain2 = norms[i]
        mod = adaln(cond8, *adas[i])[:n_groups].reshape(n_groups, 6, 1, d)
        sh1, sc1, g1, sh2, sc2, g2 = (mod[:, k] for k in range(6))
        n_kv = caches[i][0].shape[1]
        lam_init = 0.8 - 0.6 * math.exp(-0.3 * i)

        h = modulate_rows(x, gain1, sh1, sc1, n_ctx, t_lat)
        qkv = rw_matmul(h, (w_qkv,), F32, tm=1024, tn=1024, **rows_of)
        ctx_mode = "mha" if mode == "na" else mode
        ctx_params = () if mode == "na" else params
        o, new_k, new_v = ctx_attention(qkv, ctx_params, ctx_mode, lam_init, n_batch, seq, n_heads, n_kv, m)
        if mode == "na":
            o = na_latent_attention(o, qkv, *caches[i], params[0], n_ctx, bs, t_lat, n_heads)
        else:
            o = latent_attention(o, qkv, *caches[i], params, mode, lam_init, n_ctx, bs, t_lat, n_heads, n_kv)
        new_state += [new_k, new_v]
        x = rw_matmul(o, (w_out,), F32, tm=1024, tn=1024, epilogue="residual", res=x, gate=g1, **rows_of)

        if i % 2 == 0:
            w_gate, w_up, w_down = ffns[i]
            h = modulate_rows(x, gain2, sh2, sc2, n_ctx, t_lat)
            u = rw_matmul(h, (w_gate, w_up), BF16, tm=1024, tn=512, epilogue="swiglu", **rows_of)
            x = rw_matmul(u, (w_down,), F32, tm=512, tn=512, epilogue="residual", res=x, gate=g2,
                          w_buffers=1, **rows_of)
        else:
            x = moe_layer(x, gain2, sh2, sc2, g2, *ffns[i], n_ctx, t_lat)

    y_ctx = final_rmsnorm(x, final_gain, 0, n_ctx)
    y_lat = final_rmsnorm(x, final_gain, n_ctx, bs * t_lat)
    return (y_ctx.reshape(n_batch, seq, d), y_lat.reshape(bs, t_lat, d), *new_state)
```

```python
import functools
import math

import numpy as np
import jax
import jax.numpy as jnp
from jax import lax
from jax.experimental import pallas as pl
from jax.experimental.pallas import tpu as pltpu

F32 = jnp.float32
BF16 = jnp.bfloat16

HEAD_DIM = 128
GRID_W = 64
NA_ROWS = 8
NA_COLS = 16
NA_QROWS = 4
NA_BAND = 12
TOP_K = 2
ROPE_THETA = 10000.0
NORM_EPS = 1e-6
MASK_VALUE = -1e30
LOG2E = math.log2(math.e)
VMEM_LIMIT_BYTES = 56 * 2**20
MOE_TILE = 256
CAST_ROWS = 256
KEY_CHUNK = 512
CTX_HEADS_PER_STEP = 8
ROW_DMA_UNROLL = 8

def _cparams(*sem):
    return pltpu.CompilerParams(dimension_semantics=sem, vmem_limit_bytes=VMEM_LIMIT_BYTES)


def _pick(n, pref, mult=128):
    t = min(pref, n)
    t -= t % mult
    while t > mult and n % t:
        t -= mult
    assert t >= mult and n % t == 0, (n, pref)
    return t


def _dot(a, b):
    return jnp.dot(a, b, preferred_element_type=F32)


def _dot_nt(a, b):
    return lax.dot_general(a, b, (((1,), (1,)), ((), ())), preferred_element_type=F32)


def _rms(x):
    return x * lax.rsqrt(jnp.mean(x * x, axis=-1, keepdims=True) + NORM_EPS)


def _modulate(x, gain, shift, scale):
    return _rms(x) * gain * (1.0 + scale) + shift


def _group_of_tile(i, tm, n_ctx, t_lat):
    row = i * tm
    return jnp.where(row < n_ctx, 0, 1 + (row - n_ctx) // t_lat)


def _stage_bf16(src_ref, dst_ref):
    rows = math.gcd(src_ref.shape[0], CAST_ROWS)

    def body(i, c):
        r = pl.multiple_of(i * rows, rows)
        dst_ref[pl.ds(r, rows), :] = src_ref[pl.ds(r, rows), :].astype(BF16)
        return c

    lax.fori_loop(0, src_ref.shape[0] // rows, body, 0)


def _adaln_kernel(c_ref, w_ref, b_ref, o_ref):
    c = c_ref[...]
    s = c * jax.nn.sigmoid(c)
    o_ref[...] = _dot(s.astype(BF16), w_ref[...].astype(BF16)) + b_ref[...]


def adaln(cond8, w, b):
    d, n = w.shape
    tn = _pick(n, 1024)
    return pl.pallas_call(
        _adaln_kernel,
        grid=(n // tn,),
        in_specs=[pl.BlockSpec((8, d), lambda j: (0, 0)),
                  pl.BlockSpec((d, tn), lambda j: (0, j)),
                  pl.BlockSpec((1, tn), lambda j: (0, j))],
        out_specs=pl.BlockSpec((8, tn), lambda j: (0, j)),
        out_shape=jax.ShapeDtypeStruct((8, n), F32),
        compiler_params=_cparams("parallel"),
        name="adaln",
    )(cond8, w, b.reshape(1, n))


def _modulate_kernel(x_ref, g_ref, sh_ref, sc_ref, o_ref):
    o_ref[...] = _modulate(x_ref[...], g_ref[...], sh_ref[0], sc_ref[0]).astype(o_ref.dtype)


def modulate_rows(x, gain, shift, scale, n_ctx, t_lat, tm=512):
    m, d = x.shape
    tm = _pick(math.gcd(n_ctx, t_lat), tm)
    grp = lambda i: (_group_of_tile(i, tm, n_ctx, t_lat), 0, 0)
    return pl.pallas_call(
        _modulate_kernel,
        grid=(m // tm,),
        in_specs=[pl.BlockSpec((tm, d), lambda i: (i, 0)),
                  pl.BlockSpec((1, d), lambda i: (0, 0)),
                  pl.BlockSpec((1, 1, d), grp),
                  pl.BlockSpec((1, 1, d), grp)],
        out_specs=pl.BlockSpec((tm, d), lambda i: (i, 0)),
        out_shape=jax.ShapeDtypeStruct((m, d), BF16),
        compiler_params=_cparams("parallel"),
        name="modulate",
    )(x, gain.reshape(1, d), shift, scale)


def _rw_matmul_kernel(*refs, n_w, epilogue):
    a_ref = refs[0]
    w_refs = refs[1:1 + n_w]
    extra = refs[1 + n_w:-1 - n_w]
    o_ref = refs[-1 - n_w]
    wb_refs = refs[-n_w:]

    @pl.when(pl.program_id(1) == 0)
    def _():
        for w_ref, wb_ref in zip(w_refs, wb_refs):
            _stage_bf16(w_ref, wb_ref)

    a = a_ref[...]
    if epilogue == "swiglu":
        g = _dot(a, wb_refs[0][...])
        u = _dot(a, wb_refs[1][...])
        o = g * jax.nn.sigmoid(g) * u
    elif epilogue == "residual":
        res_ref, gate_ref = extra
        o = res_ref[...] + gate_ref[0] * _dot(a, wb_refs[0][...])
    else:
        o = _dot(a, wb_refs[0][...])
    o_ref[...] = o.astype(o_ref.dtype)


def rw_matmul(a, ws, out_dtype, tm, tn, epilogue="none", res=None, gate=None, n_ctx=None, t_lat=None,
              w_buffers=2):
    m, kdim = a.shape
    n = ws[0].shape[1]
    tm = _pick(math.gcd(n_ctx, t_lat) if n_ctx else m, tm)
    tn = _pick(n, tn)
    in_specs = [pl.BlockSpec((tm, kdim), lambda j, i: (i, 0))]
    in_specs += [pl.BlockSpec((kdim, tn), lambda j, i: (0, j), pipeline_mode=pl.Buffered(w_buffers)) for _ in ws]
    args = [a, *ws]
    if epilogue == "residual":
        in_specs += [pl.BlockSpec((tm, tn), lambda j, i: (i, j)),
                     pl.BlockSpec((1, 1, tn), lambda j, i: (_group_of_tile(i, tm, n_ctx, t_lat), 0, j))]
        args += [res, gate]
    return pl.pallas_call(
        functools.partial(_rw_matmul_kernel, n_w=len(ws), epilogue=epilogue),
        grid=(n // tn, m // tm),
        in_specs=in_specs,
        out_specs=pl.BlockSpec((tm, tn), lambda j, i: (i, j)),
        out_shape=jax.ShapeDtypeStruct((m, n), out_dtype),
        scratch_shapes=[pltpu.VMEM((kdim, tn), BF16) for _ in ws],
        compiler_params=_cparams("parallel", "arbitrary"),
        name="matmul_" + epilogue,
    )(*args)


def _softmax_rows(s):
    p = jnp.exp(s - jnp.max(s, axis=-1, keepdims=True))
    return p * (1.0 / jnp.sum(p, axis=-1, keepdims=True))


def _diff_lambda(lq1_ref, lk1_ref, lq2_ref, lk2_ref, lam_init):
    a = jnp.sum(lq1_ref[...] * lk1_ref[...], axis=-1, keepdims=True)
    b = jnp.sum(lq2_ref[...] * lk2_ref[...], axis=-1, keepdims=True)
    return jnp.exp(a) - jnp.exp(b) + lam_init


def _split_maps(q, scale):
    first = lax.broadcasted_iota(jnp.int32, (1, HEAD_DIM), 1) < HEAD_DIM // 2
    q = q * scale
    return jnp.where(first, q, 0.0).astype(BF16), jnp.where(first, 0.0, q).astype(BF16)


def _diff_attend(q, k_bf, v_bf, lam, subln, lam_init):
    q1, q2 = _split_maps(q, 1.0)
    scale = (HEAD_DIM // 2) ** -0.5
    p1 = _softmax_rows(_dot_nt(q1, k_bf) * scale)
    p2 = _softmax_rows(_dot_nt(q2, k_bf) * scale)
    o = _dot((p1 - lam * p2).astype(BF16), v_bf)
    return _rms(o) * subln * (1.0 - lam_init)


def _plain_attend(q_bf, k_bf, v_bf):
    s = _dot_nt(q_bf, k_bf) * (HEAD_DIM ** -0.5)
    p = jnp.exp(s - jnp.max(s, axis=-1, keepdims=True))
    return _dot(p.astype(BF16), v_bf) * (1.0 / jnp.sum(p, axis=-1, keepdims=True))


def _online_softmax_pv(q_bf, k_ref, vaug_ref, chunks):
    nq = q_bf.shape[0]
    m = jnp.full((nq, 1), -jnp.inf, F32)
    acc = jnp.zeros((nq, 2 * HEAD_DIM), F32)
    for start, size in chunks:
        s = _dot_nt(q_bf, k_ref[start:start + size, :])
        m_new = jnp.maximum(m, jnp.max(s, axis=-1, keepdims=True))
        p = jnp.exp2(s - m_new)
        acc = acc * jnp.exp2(m - m_new) +_dot(p.astype(BF16), vaug_ref[start:start + size, :])
        m = m_new
    return acc


def _rope(x, cos, sin, quarter):
    lane = lax.broadcasted_iota(jnp.int32, (1, HEAD_DIM), 1)
    even = (lane // quarter) % 2 == 0
    nxt = pltpu.roll(x, HEAD_DIM - quarter, 1)
    prv = pltpu.roll(x, quarter, 1)
    return x * cos + jnp.where(even, -nxt, prv) * sin


def _rope_tables(t_lat, dim):
    quarter = dim // 4
    inv_freq = ROPE_THETA ** (-jnp.arange(quarter, dtype=F32) / quarter)
    t = jnp.arange(t_lat)
    ang_r = (t // GRID_W).astype(F32)[:, None] * inv_freq
    ang_c = (t % GRID_W).astype(F32)[:, None] * inv_freq
    ang = jnp.concatenate([ang_r, ang_r, ang_c, ang_c] * (HEAD_DIM // dim), axis=-1)
    return jnp.cos(ang), jnp.sin(ang)


def _ctx_attn_kernel(*refs, mode, lam_init, n_q, n_kv):
    q_ref, k_ref, v_ref = refs[:3]
    o_ref, nk_ref, nv_ref = refs[-3:]
    heads = lambda ref, i: ref[:, i * HEAD_DIM:(i + 1) * HEAD_DIM]
    ks, vs = [], []
    for j in range(n_kv):
        k, v = heads(k_ref, j), heads(v_ref, j)
        if mode == "gqa":
            k = _rms(k) * refs[4][...]
        nk_ref[j] = k
        nv_ref[j] = v
        ks.append(k.astype(BF16))
        vs.append(v.astype(BF16))
    if mode == "diff":
        lam = _diff_lambda(*refs[3:7], lam_init)
    for i in range(n_q):
        q = heads(q_ref, i)
        k_bf, v_bf = ks[i % n_kv], vs[i % n_kv]
        if mode == "gqa":
            q = _rms(q) * refs[3][...]
        if mode == "diff":
            o = _diff_attend(q, k_bf, v_bf, lam, refs[7][...], lam_init)
        else:
            o = _plain_attend(q.astype(BF16), k_bf, v_bf)
        o_ref[:, i * HEAD_DIM:(i + 1) * HEAD_DIM] = o.astype(o_ref.dtype)


def ctx_attention(qkv, params, mode, lam_init, n_batch, seq, n_heads, n_kv, m_total):
    grp = n_heads // n_kv
    n_q = grp if grp > 1 else math.gcd(CTX_HEADS_PER_STEP, n_heads)
    n_k = 1 if grp > 1 else n_q
    qw, kw = n_q * HEAD_DIM, n_k * HEAD_DIM
    k_off = n_heads * HEAD_DIM // kw
    v_off = (n_heads + n_kv) * HEAD_DIM // kw
    small = [pl.BlockSpec((1, p.shape[-1]), lambda b, g: (0, 0)) for p in params]
    state = pl.BlockSpec((None, n_k, seq, HEAD_DIM), lambda b, g: (b, g, 0, 0))
    st_shape = jax.ShapeDtypeStruct((n_batch, n_kv, seq, HEAD_DIM), F32)
    return pl.pallas_call(
        functools.partial(_ctx_attn_kernel, mode=mode, lam_init=lam_init, n_q=n_q, n_kv=n_k),
        grid=(n_batch, n_heads // n_q),
        in_specs=[pl.BlockSpec((seq, qw), lambda b, g: (b, g)),
                  pl.BlockSpec((seq, kw), lambda b, g: (b, k_off + g)),
                  pl.BlockSpec((seq, kw), lambda b, g: (b, v_off + g))] + small,
        out_specs=[pl.BlockSpec((seq, qw), lambda b, g: (b, g)), state, state],
        out_shape=[jax.ShapeDtypeStruct((m_total, n_heads * HEAD_DIM), BF16), st_shape, st_shape],
        compiler_params=_cparams("parallel", "parallel"),
        name="ctx_attention_" + mode,
    )(qkv, qkv, qkv, *[p.reshape(1, -1) for p in params])


def _lat_attn_kernel(*refs, mode, lam_init, n_ctx_keys, grp, quarter, chunks):
    (o_in_ref, q_ref, kl_ref, vl_ref, ck_ref, cv_ref,
     cosq_ref, sinq_ref, cosk_ref, sink_ref) = refs[:10]
    o_ref, kall_ref, vaug_ref = refs[-3:]
    del o_in_ref
    params = refs[10:-3]
    h, qi = pl.program_id(1), pl.program_id(2)

    @pl.when(jnp.logical_and(qi == 0, h % grp == 0))
    def _():
        k = kl_ref[...]
        if mode == "gqa":
            k = _rms(k) * params[1][...]
        kall_ref[:n_ctx_keys, :] = ck_ref[...].astype(BF16)
        kall_ref[n_ctx_keys:, :] = _rope(k, cosk_ref[...], sink_ref[...], quarter).astype(BF16)
        vaug_ref[:n_ctx_keys, :HEAD_DIM] = cv_ref[...].astype(BF16)
        vaug_ref[n_ctx_keys:, :HEAD_DIM] = vl_ref[...].astype(BF16)
        vaug_ref[:, HEAD_DIM:] = jnp.ones((vaug_ref.shape[0], HEAD_DIM), BF16)

    q = q_ref[...]
    if mode == "gqa":
        q = _rms(q) * params[0][...]
    q = _rope(q, cosq_ref[...], sinq_ref[...], quarter)
    if mode == "diff":
        lam = _diff_lambda(*params[:4], lam_init)
        q1, q2 = _split_maps(q, (HEAD_DIM // 2) ** -0.5 * LOG2E)
        a1 = _online_softmax_pv(q1, kall_ref, vaug_ref, chunks)
        a2 = _online_softmax_pv(q2, kall_ref, vaug_ref, chunks)
        o = a1[:, :HEAD_DIM] * (1.0 / a1[:, HEAD_DIM:]) - a2[:, :HEAD_DIM] * (lam / a2[:, HEAD_DIM:])
        o = _rms(o) * params[4][...] * (1.0 - lam_init)
    else:
        a = _online_softmax_pv((q * (HEAD_DIM ** -0.5 * LOG2E)).astype(BF16), kall_ref, vaug_ref, chunks)
        o = a[:, :HEAD_DIM] * (1.0 / a[:, HEAD_DIM:])
    o_ref[...] = o.astype(o_ref.dtype)


def latent_attention(o_ctx, qkv, cache_k, cache_v, params, mode, lam_init,
                     n_ctx, n_batch, t_lat, n_heads, n_kv, tq=1024):
    grp = n_heads // n_kv
    n_past = cache_k.shape[2]
    tq = _pick(math.gcd(n_ctx, t_lat), tq)
    nq = t_lat // tq
    assert n_ctx % t_lat == 0
    base_q, base_k = n_ctx // tq, n_ctx // t_lat
    dim = HEAD_DIM // 2 if mode == "diff" else HEAD_DIM
    cos, sin = _rope_tables(t_lat, dim)
    kc = _pick(t_lat, KEY_CHUNK)
    chunks = ((0, n_past),) + tuple((n_past + i * kc, kc) for i in range(t_lat // kc))
    qblk = pl.BlockSpec((tq, HEAD_DIM), lambda b, h, i: (base_q + b * nq + i, h))
    kvblk = lambda off: pl.BlockSpec((t_lat, HEAD_DIM), lambda b, h, i: (base_k + b, off + h // grp))
    cblk = pl.BlockSpec((None, None, n_past, HEAD_DIM), lambda b, h, i: (b, h // grp, 0, 0))
    tabq = pl.BlockSpec((tq, HEAD_DIM), lambda b, h, i: (i, 0))
    tabk = pl.BlockSpec((t_lat, HEAD_DIM), lambda b, h, i: (0, 0))
    small = [pl.BlockSpec((1, p.shape[-1]), lambda b, h, i: (0, 0)) for p in params]
    return pl.pallas_call(
        functools.partial(_lat_attn_kernel, mode=mode, lam_init=lam_init,
                          n_ctx_keys=n_past, grp=grp, quarter=dim // 4, chunks=chunks),
        grid=(n_batch, n_heads, nq),
        in_specs=[pl.BlockSpec(memory_space=pl.ANY), qblk, kvblk(n_heads), kvblk(n_heads + n_kv),
                  cblk, cblk, tabq, tabq, tabk, tabk] + small,
        out_specs=qblk,
        out_shape=jax.ShapeDtypeStruct(o_ctx.shape, o_ctx.dtype),
        scratch_shapes=[pltpu.VMEM((n_past + t_lat, HEAD_DIM), BF16),
                        pltpu.VMEM((n_past + t_lat, 2 * HEAD_DIM), BF16)],
        input_output_aliases={0: 0},
        compiler_params=_cparams("parallel", "arbitrary", "arbitrary"),
        name="latent_attention_" + mode,
    )(o_ctx, qkv, qkv, qkv, cache_k, cache_v, cos, sin, cos, sin,
      *[p.reshape(1, -1) for p in params])


def _na_geometry(rows):
    kh = min(NA_ROWS, rows)
    band = min(NA_BAND, rows)
    nblk = rows // NA_QROWS
    qr = np.arange(NA_QROWS)[:, None]
    kr = np.arange(band)[None, :]
    qc = np.arange(GRID_W)[:, None]
    kc = np.arange(GRID_W)[None, :]
    cs = np.clip(qc - NA_COLS // 2, 0, GRID_W - NA_COLS)
    col_ok = (kc >= cs) & (kc < cs + NA_COLS)
    col_idx = np.where(col_ok, kc - qc + NA_COLS - 1, 0)
    starts, pat_ids, pats, keys = [], [], [], []
    for j in range(nblk):
        r0 = j * NA_QROWS
        bs = int(np.clip(r0 - kh // 2, 0, rows - band))
        r = r0 + qr
        rs = np.clip(r - kh // 2, 0, rows - kh)
        key_row = bs + kr
        row_ok = (key_row >= rs) & (key_row < rs + kh)
        row_idx = np.where(row_ok, key_row - r + NA_ROWS - 1, 0)
        key = (bs - r0, tuple((rs - r0).ravel()))
        if key not in keys:
            keys.append(key)
            pats.append((row_ok[:, None, :, None] & col_ok[None, :, None, :], row_idx))
        starts.append(bs)
        pat_ids.append(keys.index(key))
    return starts, pat_ids, pats, col_idx


def _na_bias_tables(rpb, rows):
    starts, pat_ids, pats, col_idx = _na_geometry(rows)
    n_heads = rpb.shape[0]
    col_onehot = jnp.asarray(col_idx[:, :, None] == np.arange(rpb.shape[2]), F32)
    tables = []
    for valid, row_idx in pats:
        by_row = rpb[:, row_idx, :]
        full = jnp.einsum("hqkj,cdj->hqckd", by_row, col_onehot, precision=lax.Precision.HIGHEST)
        full = jnp.where(valid[None], full, MASK_VALUE)
        tables.append(full.reshape(n_heads, valid.shape[0] * GRID_W, valid.shape[2] * GRID_W))
    return starts, pat_ids, jnp.stack(tables)


def _na_attn_kernel(start_ref, pat_ref, o_in_ref, q_ref, kl_ref, vl_ref, ck_ref, cv_ref, bias_ref,
                    o_ref, *, band_keys):
    del pat_ref, o_in_ref
    start = pl.multiple_of(start_ref[pl.program_id(2)] * GRID_W, GRID_W)
    scale = HEAD_DIM ** -0.5
    q = q_ref[...].astype(BF16)
    kb = kl_ref[pl.ds(start, band_keys), :].astype(BF16)
    vb = vl_ref[pl.ds(start, band_keys), :].astype(BF16)
    s_nb = _dot_nt(q, kb) * scale + bias_ref[...]
    s_cx = _dot_nt(q, ck_ref[...].astype(BF16)) * scale
    m = jnp.maximum(jnp.max(s_nb, axis=-1, keepdims=True), jnp.max(s_cx, axis=-1, keepdims=True))
    p_nb = jnp.exp(s_nb - m)
    p_cx = jnp.exp(s_cx - m)
    l = jnp.sum(p_nb, axis=-1, keepdims=True) + jnp.sum(p_cx, axis=-1, keepdims=True)
    o = _dot(p_nb.astype(BF16), vb) + _dot(p_cx.astype(BF16), cv_ref[...].astype(BF16))
    o_ref[...] = (o * (1.0 / l)).astype(o_ref.dtype)


def na_latent_attention(o_ctx, qkv, cache_k, cache_v, rpb, n_ctx, n_batch, t_lat, n_heads):
    rows = t_lat // GRID_W
    assert rows % NA_QROWS == 0 and n_ctx % t_lat == 0
    starts, pat_ids, bias = _na_bias_tables(rpb, rows)
    nblk = len(starts)
    tq = NA_QROWS * GRID_W
    band_keys = min(NA_BAND, rows) * GRID_W
    n_past = cache_k.shape[2]
    base_q, base_k = n_ctx // tq, n_ctx // t_lat
    qblk = pl.BlockSpec((tq, HEAD_DIM), lambda b, h, j, st, pt: (base_q + b * nblk + j, h))
    kvblk = lambda off: pl.BlockSpec((t_lat, HEAD_DIM), lambda b, h, j, st, pt: (base_k + b, off + h))
    cblk = pl.BlockSpec((None, None, n_past, HEAD_DIM), lambda b, h, j, st, pt: (b, h, 0, 0))
    bblk = pl.BlockSpec((None, None, tq, band_keys), lambda b, h, j, st, pt: (pt[j], h, 0, 0))
    return pl.pallas_call(
        functools.partial(_na_attn_kernel, band_keys=band_keys),
        grid_spec=pltpu.PrefetchScalarGridSpec(
            num_scalar_prefetch=2,
            grid=(n_batch, n_heads, nblk),
            in_specs=[pl.BlockSpec(memory_space=pl.ANY), qblk, kvblk(n_heads), kvblk(2 * n_heads),
                      cblk, cblk, bblk],
            out_specs=qblk),
        out_shape=jax.ShapeDtypeStruct(o_ctx.shape, o_ctx.dtype),
        input_output_aliases={2: 0},
        compiler_params=_cparams("parallel", "arbitrary", "arbitrary"),
        name="latent_attention_na",
    )(jnp.asarray(starts, jnp.int32), jnp.asarray(pat_ids, jnp.int32),
      o_ctx, qkv, qkv, qkv, cache_k, cache_v, bias)


def _router_kernel(x_ref, g_ref, sh_ref, sc_ref, rt_ref, h_ref, idx_ref, gate_ref):
    h = _modulate(x_ref[...], g_ref[...], sh_ref[0], sc_ref[0])
    h_ref[...] = h
    logits = lax.dot_general(rt_ref[...], h, (((1,), (1,)), ((), ())),
                             precision=lax.Precision.HIGHEST, preferred_element_type=F32)
    n_exp = logits.shape[0]
    eid = lax.broadcasted_iota(jnp.int32, logits.shape, 0)
    m1 = jnp.max(logits, axis=0, keepdims=True)
    i1 = jnp.min(jnp.where(logits == m1, eid, n_exp), axis=0, keepdims=True)
    rest = jnp.where(eid == i1, -jnp.inf, logits)
    m2 = jnp.max(rest, axis=0, keepdims=True)
    i2 = jnp.min(jnp.where(rest == m2, eid, n_exp), axis=0, keepdims=True)
    e = jnp.exp(m2 - m1)
    idx_ref[0:1, :] = i1
    idx_ref[1:2, :] = i2
    gate_ref[0:1, :] = 1.0 / (1.0 + e)
    gate_ref[1:2, :] = e / (1.0 + e)


def moe_route(x, gain, shift, scale, router, n_ctx, t_lat, tm=512):
    m, d = x.shape
    n_exp = router.shape[1]
    tm = _pick(math.gcd(n_ctx, t_lat), tm)
    grp = lambda i: (_group_of_tile(i, tm, n_ctx, t_lat), 0, 0)
    return pl.pallas_call(
        _router_kernel,
        grid=(m // tm,),
        in_specs=[pl.BlockSpec((tm, d), lambda i: (i, 0)),
                  pl.BlockSpec((1, d), lambda i: (0, 0)),
                  pl.BlockSpec((1, 1, d), grp),
                  pl.BlockSpec((1, 1, d), grp),
                  pl.BlockSpec((n_exp, d), lambda i: (0, 0))],
        out_specs=[pl.BlockSpec((tm, d), lambda i: (i, 0)),
                   pl.BlockSpec((TOP_K, tm), lambda i: (0, i)),
                   pl.BlockSpec((TOP_K, tm), lambda i: (0, i))],
        out_shape=[jax.ShapeDtypeStruct((m, d), F32),
                   jax.ShapeDtypeStruct((TOP_K, m), jnp.int32),
                   jax.ShapeDtypeStruct((TOP_K, m), F32)],
        compiler_params=_cparams("parallel"),
        name="moe_router",
    )(x, gain.reshape(1, d), shift, scale, router.T)


def _row_gather(idx_ref, base, src_hbm, dst_ref, sem, wait):
    def body(r, c):
        cp = pltpu.make_async_copy(src_hbm.at[pl.ds(idx_ref[base + r], 1)], dst_ref.at[pl.ds(r, 1)], sem)
        if wait:
            cp.wait()
        else:
            cp.start()
        return c

    lax.fori_loop(0, dst_ref.shape[0], body, 0, unroll=ROW_DMA_UNROLL)


def _gather_rows_kernel(src_ref, h_hbm, o_ref, buf_ref, sem):
    t = pl.program_id(0)
    rows = buf_ref.shape[1]
    slot = lax.rem(t, 2)

    @pl.when(t == 0)
    def _():
        _row_gather(src_ref, 0, h_hbm, buf_ref.at[0], sem.at[0], wait=False)

    @pl.when(t + 1 < pl.num_programs(0))
    def _():
        _row_gather(src_ref, (t + 1) * rows, h_hbm, buf_ref.at[1 - slot], sem.at[1 - slot], wait=False)

    _row_gather(src_ref, t * rows, h_hbm, buf_ref.at[slot], sem.at[slot], wait=True)
    o_ref[...] = buf_ref[slot].astype(o_ref.dtype)


def gather_rows(h, src, n_tiles):
    d = h.shape[1]
    return pl.pallas_call(
        _gather_rows_kernel,
        grid_spec=pltpu.PrefetchScalarGridSpec(
            num_scalar_prefetch=1,
            grid=(n_tiles,),
            in_specs=[pl.BlockSpec(memory_space=pl.ANY)],
            out_specs=pl.BlockSpec((MOE_TILE, d), lambda t, s: (t, 0)),
            scratch_shapes=[pltpu.VMEM((2, MOE_TILE, d), h.dtype), pltpu.SemaphoreType.DMA((2,))]),
        out_shape=jax.ShapeDtypeStruct((n_tiles * MOE_TILE, d), BF16),
        compiler_params=_cparams("arbitrary"),
        name="moe_gather",
    )(src, h)


def _grouped_matmul_kernel(first_ref, count_ref, x_hbm, *refs, n_w, swiglu, n_tiles, tile):
    w_refs = refs[:n_w]
    o_hbm = refs[n_w]
    wb_refs = refs[n_w + 1:2 * n_w + 1]
    xbuf, obuf, xtail, otail, in_sem, out_sem, tail_in_sem, tail_out_sem = refs[2 * n_w + 1:]
    j, e = pl.program_id(0), pl.program_id(1)
    first, count = first_ref[e], count_ref[e]
    tn = obuf.shape[2]
    big = obuf.shape[1] // tile
    n_big = count // big
    n_tail = count - n_big * big
    tail0 = first + n_big * big
    col0 = pl.multiple_of(j * tn, tn)

    def rows(t, n):
        return pl.ds(pl.multiple_of(t * tile, tile), n * tile)

    def in_copy(c, slot):
        return pltpu.make_async_copy(x_hbm.at[rows(first + c * big, big)], xbuf.at[slot], in_sem.at[slot])

    def out_copy(c, slot):
        return pltpu.make_async_copy(obuf.at[slot], o_hbm.at[rows(first + c * big, big), pl.ds(col0, tn)],
                                     out_sem.at[slot])

    def tail_in(s):
        return pltpu.make_async_copy(x_hbm.at[rows(tail0 + s, 1)], xtail.at[s], tail_in_sem.at[s])

    def tail_out(s):
        return pltpu.make_async_copy(otail.at[s], o_hbm.at[rows(tail0 + s, 1), pl.ds(col0, tn)],
                                     tail_out_sem.at[s])

    def apply_weights(x):
        if swiglu:
            a = _dot(x, wb_refs[0][...])
            b = _dot(x, wb_refs[1][...])
            return (a * jax.nn.sigmoid(a) * b).astype(obuf.dtype)
        return _dot(x, wb_refs[0][...]).astype(obuf.dtype)

    @pl.when(count > 0)
    def _():
        @pl.when(n_big > 0)
        def _():
            in_copy(0, 0).start()

        for s in range(big - 1):
            @pl.when(s < n_tail)
            def _():
                tail_in(s).start()

        for w_ref, wb_ref in zip(w_refs, wb_refs):
            _stage_bf16(w_ref, wb_ref)

        def chunk_pair(pair, carry):
            for slot in (0, 1):
                c = 2 * pair + slot

                @pl.when(c < n_big)
                def _():
                    in_copy(c, slot).wait()

                    @pl.when(c + 1 < n_big)
                    def _():
                        in_copy(c + 1, 1 - slot).start()

                    @pl.when(c >= 2)
                    def _():
                        out_copy(c - 2, slot).wait()

                    obuf[slot] = apply_weights(xbuf[slot])
                    out_copy(c, slot).start()
            return carry

        lax.fori_loop(0, (n_big + 1) // 2, chunk_pair, 0)

        for s in range(big - 1):
            @pl.when(s < n_tail)
            def _():
                tail_in(s).wait()
                otail[s] = apply_weights(xtail[s])
                tail_out(s).start()

        for back in (2, 1):
            @pl.when(n_big >= back)
            def _():
                c = n_big - back
                out_copy(c, lax.rem(c, 2)).wait()

        for s in range(big - 1):
            @pl.when(s < n_tail)
            def _():
                tail_out(s).wait()

    @pl.when(e == pl.num_programs(1) - 1)
    def _():
        otail[0] = jnp.zeros((tile, tn), otail.dtype)

        def zero_tile(t, carry):
            cp = pltpu.make_async_copy(otail.at[0], o_hbm.at[rows(t, 1), pl.ds(col0, tn)], tail_out_sem.at[0])
            cp.start()
            cp.wait()
            return carry

        lax.fori_loop(first + count, n_tiles, zero_tile, 0)


def grouped_matmul(xs, ws, tile_first, tile_count, out_dtype, tn, chunk_tiles, swiglu=False, w_buffers=2):
    p, kdim = xs.shape
    n_exp, _, n = ws[0].shape
    tn = _pick(n, tn)
    n_tiles = p // MOE_TILE
    chunk_rows = chunk_tiles * MOE_TILE
    n_tail_bufs = max(chunk_tiles - 1, 1)
    wblk =pl.BlockSpec((None, kdim, tn), lambda j, e, first, count: (e, 0, j),
                        pipeline_mode=pl.Buffered(w_buffers))
    return pl.pallas_call(
        functools.partial(_grouped_matmul_kernel, n_w=len(ws), swiglu=swiglu, n_tiles=n_tiles, tile=MOE_TILE),
        grid_spec=pltpu.PrefetchScalarGridSpec(
            num_scalar_prefetch=2,
            grid=(n // tn, n_exp),
            in_specs=[pl.BlockSpec(memory_space=pl.ANY)] + [wblk for _ in ws],
            out_specs=pl.BlockSpec(memory_space=pl.ANY),
            scratch_shapes=[pltpu.VMEM((kdim, tn), BF16) for _ in ws]
                           + [pltpu.VMEM((2, chunk_rows, kdim), xs.dtype),
                              pltpu.VMEM((2, chunk_rows, tn), out_dtype),
                              pltpu.VMEM((n_tail_bufs, MOE_TILE, kdim), xs.dtype),
                              pltpu.VMEM((n_tail_bufs, MOE_TILE, tn), out_dtype),
                              pltpu.SemaphoreType.DMA((2,)),
                              pltpu.SemaphoreType.DMA((2,)),
                              pltpu.SemaphoreType.DMA((n_tail_bufs,)),
                              pltpu.SemaphoreType.DMA((n_tail_bufs,))]),
        out_shape=jax.ShapeDtypeStruct((p, n), out_dtype),
        compiler_params=_cparams("arbitrary", "arbitrary"),
        name="moe_expert_swiglu" if swiglu else "moe_expert_down",
    )(tile_first, tile_count, xs, *ws)


_COMBINE_POST_IO = {None: (0, 0), "modulate": (3, 1), "final_norm": (1, 2)}


def _combine_kernel(p0_ref, p1_ref, x_ref, g_ref, ga_ref, gb_ref, *refs, post, n_ctx_tiles):
    n_in, n_out = _COMBINE_POST_IO[post]
    post_in = refs[:n_in]
    y_hbm, o_ref = refs[n_in], refs[n_in + 1]
    post_out = refs[n_in + 2:n_in + 2 + n_out]
    ya_ref, yb_ref, sem = refs[n_in + 2 + n_out:]
    i = pl.program_id(0)
    rows = ya_ref.shape[1]
    slot = lax.rem(i, 2)

    def gather(tile, s, wait):
        _row_gather(p0_ref, tile * rows, y_hbm, ya_ref.at[s], sem.at[0, s], wait)
        _row_gather(p1_ref, tile * rows, y_hbm, yb_ref.at[s], sem.at[1, s], wait)

    @pl.when(i == 0)
    def _():
        gather(0, 0, False)

    @pl.when(i + 1 < pl.num_programs(0))
    def _():
        gather(i + 1, 1 - slot, False)

    gather(i, slot, True)
    x_new = x_ref[...] + g_ref[0] * (ya_ref[slot] * ga_ref[...] + yb_ref[slot] * gb_ref[...])
    o_ref[...] = x_new
    if post == "modulate":
        gain_ref, sh_ref, sc_ref = post_in
        post_out[0][...] = _modulate(x_new, gain_ref[...], sh_ref[0], sc_ref[0]).astype(post_out[0].dtype)
    elif post == "final_norm":
        normed = _rms(x_new) * post_in[0][...]

        @pl.when(i < n_ctx_tiles)
        def _():
            post_out[0][...] = normed

        @pl.when(i >= n_ctx_tiles)
        def _():
            post_out[1][...] = normed


def moe_combine(x, gate, y, pos, route_gates, n_ctx, t_lat, post=None, post_args=(), tm=256):
    m, d = x.shape
    tm = _pick(math.gcd(n_ctx, t_lat), tm)
    nct = n_ctx // tm
    pos0, pos1 = pos[:, 0], pos[:, 1]
    ga, gb = route_gates[:, 0:1], route_gates[:, 1:2]
    row_blk = pl.BlockSpec((tm, d), lambda i, a, b: (i, 0))
    grp_blk = pl.BlockSpec((1, 1, d), lambda i, a, b: (_group_of_tile(i, tm, n_ctx, t_lat), 0, 0))
    vec_blk = pl.BlockSpec((1, d), lambda i, a, b: (0, 0))
    post_in_specs, post_in, post_out_specs, post_out_shapes = [], [], [], []
    if post == "modulate":
        gain, shift, scale = post_args
        post_in_specs, post_in = [vec_blk, grp_blk, grp_blk], [gain.reshape(1, d), shift, scale]
        post_out_specs, post_out_shapes = [row_blk], [jax.ShapeDtypeStruct((m, d), BF16)]
    elif post == "final_norm":
        post_in_specs, post_in = [vec_blk], [post_args[0].reshape(1, d)]
        post_out_specs = [pl.BlockSpec((tm, d), lambda i, a, b: (jnp.minimum(i, nct - 1), 0)),
                          pl.BlockSpec((tm, d), lambda i, a, b: (jnp.maximum(i - nct, 0), 0))]
        post_out_shapes = [jax.ShapeDtypeStruct((n_ctx, d), F32), jax.ShapeDtypeStruct((m - n_ctx, d), F32)]
    return pl.pallas_call(
        functools.partial(_combine_kernel, post=post, n_ctx_tiles=nct),
        grid_spec=pltpu.PrefetchScalarGridSpec(
            num_scalar_prefetch=2,
            grid=(m // tm,),
            in_specs=[row_blk, grp_blk,
                      pl.BlockSpec((tm, 1), lambda i, a, b: (i, 0)),
                      pl.BlockSpec((tm, 1), lambda i, a, b: (i, 0))]
                     + post_in_specs + [pl.BlockSpec(memory_space=pl.ANY)],
            out_specs=[row_blk] + post_out_specs,
            scratch_shapes=[pltpu.VMEM((2, tm, d), F32), pltpu.VMEM((2, tm, d), F32),
                            pltpu.SemaphoreType.DMA((2, 2))]),
        out_shape=[jax.ShapeDtypeStruct((m, d), F32)] + post_out_shapes,
        compiler_params=_cparams("arbitrary"),
        name="moe_combine",
    )(pos0, pos1, x, gate, ga, gb, *post_in, y)


def moe_layer(x, gain, shift, scale, gate, router, w_gate, w_up, w_down, n_ctx, t_lat, post=None, post_args=()):
    m, _ = x.shape
    n_exp = router.shape[1]
    h, idx, gates = moe_route(x, gain, shift, scale, router, n_ctx, t_lat)
    expert = idx.T.reshape(-1)
    n_assign = expert.shape[0]
    onehot = (expert[:, None] == jnp.arange(n_exp)[None, :]).astype(jnp.int32)
    csum = jnp.cumsum(onehot, axis=0)
    rank = jnp.sum(onehot * (csum - onehot), axis=1)
    counts = csum[-1]
    padded = (counts + MOE_TILE - 1) // MOE_TILE * MOE_TILE
    pad_end = jnp.cumsum(padded)
    dest = jnp.sum(onehot * (pad_end - padded)[None, :], axis=1) + rank
    n_tiles = n_assign // MOE_TILE + n_exp
    token = jnp.arange(n_assign, dtype=jnp.int32) // TOP_K
    src = jnp.zeros((n_tiles * MOE_TILE,), jnp.int32).at[dest].set(token)
    tile_first = ((pad_end - padded) // MOE_TILE).astype(jnp.int32)
    tile_count = (padded // MOE_TILE).astype(jnp.int32)
    pos = dest.reshape(m, TOP_K).astype(jnp.int32)

    xs = gather_rows(h, src, n_tiles)
    u = grouped_matmul(xs, (w_gate, w_up), tile_first, tile_count, BF16, tn=1024, chunk_tiles=2, swiglu=True)
    y = grouped_matmul(u, (w_down,), tile_first, tile_count, F32, tn=512, chunk_tiles=1)
    return moe_combine(x, gate, y, pos, gates.T, n_ctx, t_lat, post, post_args)


def _final_norm_kernel(x_ref, g_ref, o_ref):
    o_ref[...] = _rms(x_ref[...]) * g_ref[...]


def final_rmsnorm(x, gain, row0, n_rows, tm=512):
    d = x.shape[1]
    tm = _pick(math.gcd(row0, n_rows) if row0 else n_rows, tm)
    blk0 = row0 // tm
    return pl.pallas_call(
        _final_norm_kernel,
        grid=(n_rows // tm,),
        in_specs=[pl.BlockSpec((tm, d), lambda i: (blk0 + i, 0)), pl.BlockSpec((1, d), lambda i: (0, 0))],
        out_specs=pl.BlockSpec((tm, d), lambda i: (i, 0)),
        out_shape=jax.ShapeDtypeStruct((n_rows, d), F32),
        compiler_params=_cparams("parallel"),
        name="final_norm",
    )(x, gain.reshape(1, d))


def kernel(x_prompt, x_sample, c, cache_k_l0, cache_v_l0, cache_k_l1, cache_v_l1, cache_k_l2, cache_v_l2, cache_k_l3, cache_v_l3, c_ctx, final_norm, norm1_l0, norm2_l0, ada_w_l0, ada_b_l0, attn_qkv_l0, attn_out_l0, lambda_q1_l0, lambda_k1_l0, lambda_q2_l0, lambda_k2_l0, subln_l0, ffn_gate_l0, ffn_up_l0, ffn_down_l0, norm1_l1, norm2_l1, ada_w_l1, ada_b_l1, attn_qkv_l1, attn_out_l1, q_norm_l1, k_norm_l1, router_l1, moe_gate_l1, moe_up_l1, moe_down_l1, norm1_l2, norm2_l2, ada_w_l2, ada_b_l2, attn_qkv_l2, attn_out_l2, rpb_l2, ffn_gate_l2, ffn_up_l2, ffn_down_l2, norm1_l3, norm2_l3, ada_w_l3, ada_b_l3, attn_qkv_l3, attn_out_l3, lambda_q1_l3, lambda_k1_l3, lambda_q2_l3, lambda_k2_l3, subln_l3, router_l3, moe_gate_l3, moe_up_l3, moe_down_l3):
    final_gain = final_norm
    mixers = [
        ("diff", attn_qkv_l0, attn_out_l0, (lambda_q1_l0, lambda_k1_l0, lambda_q2_l0, lambda_k2_l0, subln_l0)),
        ("gqa", attn_qkv_l1, attn_out_l1, (q_norm_l1, k_norm_l1)),
        ("na", attn_qkv_l2, attn_out_l2, (rpb_l2,)),
        ("diff", attn_qkv_l3, attn_out_l3, (lambda_q1_l3, lambda_k1_l3, lambda_q2_l3, lambda_k2_l3, subln_l3)),
    ]
    ffns = [
        (ffn_gate_l0, ffn_up_l0, ffn_down_l0),
        (router_l1, moe_gate_l1, moe_up_l1, moe_down_l1),
        (ffn_gate_l2, ffn_up_l2, ffn_down_l2),
        (router_l3, moe_gate_l3, moe_up_l3, moe_down_l3),
    ]
    caches = [(cache_k_l0, cache_v_l0), (cache_k_l1, cache_v_l1),
              (cache_k_l2, cache_v_l2), (cache_k_l3, cache_v_l3)]
    norms = [(norm1_l0, norm2_l0), (norm1_l1, norm2_l1), (norm1_l2, norm2_l2), (norm1_l3, norm2_l3)]
    adas = [(ada_w_l0, ada_b_l0), (ada_w_l1, ada_b_l1), (ada_w_l2, ada_b_l2), (ada_w_l3, ada_b_l3)]

    n_batch, seq, d = x_prompt.shape
    bs, t_lat, _ = x_sample.shape
    n_ctx = n_batch * seq
    m = n_ctx + bs * t_lat
    n_heads = d // HEAD_DIM
    n_groups = 1 + bs
    assert n_groups <= 8
    rows_of = dict(n_ctx=n_ctx, t_lat=t_lat)

    x = jnp.concatenate([x_prompt.reshape(n_ctx, d), x_sample.reshape(bs * t_lat, d)], axis=0)
    cond8 = jnp.zeros((8, d), F32).at[0].set(c_ctx).at[1:n_groups].set(c)

    mods = []
    for ada in adas:
        mod = adaln(cond8, *ada)[:n_groups].reshape(n_groups, 6, 1, d)
        mods.append(tuple(mod[:, k] for k in range(6)))

    new_state = []
    h = y_ctx = y_lat = None
    for i, (mode, w_qkv, w_out, params) in enumerate(mixers):
        gain1, gain2 = norms[i]
        sh1, sc1, g1, sh2, sc2, g2 = mods[i]
        n_kv = caches[i][0].shape[1]
        lam_init = 0.8 - 0.6 * math.exp(-0.3 * i)

        if h is None:
            h = modulate_rows(x, gain1, sh1, sc1, n_ctx, t_lat)
        qkv = rw_matmul(h, (w_qkv,), F32, tm=1024, tn=1024, **rows_of)
        ctx_mode = "mha" if mode == "na" else mode
        ctx_params = () if mode == "na" else params
        o, new_k, new_v = ctx_attention(qkv, ctx_params, ctx_mode, lam_init, n_batch, seq, n_heads, n_kv, m)
        if mode == "na":
            o = na_latent_attention(o, qkv, *caches[i], params[0], n_ctx, bs, t_lat, n_heads)
        else:
            o = latent_attention(o, qkv, *caches[i], params, mode, lam_init, n_ctx, bs, t_lat, n_heads, n_kv)
        new_state += [new_k, new_v]
        x = rw_matmul(o, (w_out,), F32, tm=1024, tn=1024, epilogue="residual", res=x, gate=g1, **rows_of)

        if i % 2 == 0:
            w_gate, w_up, w_down = ffns[i]
            h = modulate_rows(x, gain2, sh2, sc2, n_ctx, t_lat)
            u = rw_matmul(h, (w_gate, w_up), BF16, tm=1024, tn=512, epilogue="swiglu", **rows_of)
            x = rw_matmul(u, (w_down,), F32, tm=512, tn=512, epilogue="residual", res=x, gate=g2,
                          w_buffers=1, **rows_of)
            h = None
        elif i + 1 < len(mixers):
            x, h = moe_layer(x, gain2, sh2, sc2, g2, *ffns[i], n_ctx, t_lat,
                             post="modulate", post_args=(norms[i + 1][0], mods[i + 1][0], mods[i + 1][1]))
        else:
            x, y_ctx, y_lat = moe_layer(x, gain2, sh2, sc2, g2, *ffns[i], n_ctx, t_lat,
                                        post="final_norm", post_args=(final_gain,))

    if y_ctx is None:
        y_ctx = final_rmsnorm(x, final_gain, 0, n_ctx)
        y_lat = final_rmsnorm(x, final_gain, n_ctx, bs * t_lat)
    return (y_ctx.reshape(n_batch, seq, d), y_lat.reshape(bs, t_lat, d), *new_state)
```
